```python
import jax, jax.numpy as jnp
from jax import lax
import numpy as np

D_MODEL = 1024
BATCH = 16
SEQ = 4096
DEPTH = 2
DEC_BATCH = 16
DEC_SEQ = 16
PAST_LEN = 2048

CHUNK = 64
Q_BLOCK = 128
D_PLE = 256
N_HEADS_A = 8
QK_NOPE = 64
QK_ROPE = 32
V_HEAD = 64
KV_LORA = 128
Q_LORA = 256
WIDTH_A = N_HEADS_A * V_HEAD
CONV_CH = D_MODEL // 2
CONV_W = 31
CONV_STATE = CONV_W - 1
D_MIX = WIDTH_A + CONV_CH
D_IN = Q_LORA + KV_LORA + QK_ROPE + WIDTH_A + 2 * CONV_CH + CONV_CH
ROPE_THETA = 10000.0
ATTN_SCALE = (QK_NOPE + QK_ROPE) ** -0.5
EPS = 1e-6
NEG_INF = -1e30

kernel_name = "hymba_mla_conformer_stream_step"


def rmsnorm(x, g):
    xf = x.astype(jnp.float32)
    y = xf * lax.rsqrt(jnp.mean(xf * xf, axis=-1, keepdims=True) + EPS)
    return (y * g.astype(jnp.float32)).astype(x.dtype)


def layernorm(x, g, b):
    xf = x.astype(jnp.float32)
    mu = jnp.mean(xf, axis=-1, keepdims=True)
    xc = xf - mu
    var = jnp.mean(xc * xc, axis=-1, keepdims=True)
    y = xc * lax.rsqrt(var + EPS) * g.astype(jnp.float32) + b.astype(jnp.float32)
    return y.astype(x.dtype)


def rope(x, pos):
    half = QK_ROPE // 2
    inv = ROPE_THETA ** (-jnp.arange(half, dtype=jnp.float32) / half)
    ang = pos.astype(jnp.float32)[:, None] * inv[None, :]
    cos = jnp.cos(ang)[None, :, None, :]
    sin = jnp.sin(ang)[None, :, None, :]
    x1 = x[..., :half].astype(jnp.float32)
    x2 = x[..., half:].astype(jnp.float32)
    return jnp.concatenate([x1 * cos - x2 * sin, x1 * sin + x2 * cos], axis=-1).astype(x.dtype)


def mla_attend(q_lat, q_rope, ckv, krope, mask):
    s = (jnp.einsum('bthr,bsr->bhts', q_lat, ckv)
         + jnp.einsum('bthe,bse->bhts', q_rope, krope)).astype(jnp.float32) * ATTN_SCALE
    if mask is not None:
        s = jnp.where(mask[None, None], s, NEG_INF)
    p = jax.nn.softmax(s, axis=-1).astype(ckv.dtype)
    return jnp.einsum('bhts,bsr->bthr', p, ckv)


def prompt_attention(q_lat, q_rope, ckv, krope):
    b, t = q_lat.shape[0], q_lat.shape[1]
    nb = t // Q_BLOCK
    ql = q_lat.reshape(b, nb, Q_BLOCK, N_HEADS_A, KV_LORA).transpose(1, 0, 2, 3, 4)
    qr = q_rope.reshape(b, nb, Q_BLOCK, N_HEADS_A, QK_ROPE).transpose(1, 0, 2, 3, 4)
    k_chunk = jnp.arange(t) // CHUNK

    def block(args):
        ql_b, qr_b, i = args
        q_chunk = (i * Q_BLOCK + jnp.arange(Q_BLOCK)) // CHUNK
        mask = k_chunk[None, :] <= q_chunk[:, None]
        return mla_attend(ql_b, qr_b, ckv, krope, mask)

    o = lax.map(block, (ql, qr, jnp.arange(nb)))
    return o.transpose(1, 0, 2, 3, 4).reshape(b, t, N_HEADS_A, KV_LORA)


def causal_dwconv(u_pad, w, bias):
    y = lax.conv_general_dilated(u_pad, w[:, None, :], window_strides=(1,), padding='VALID',
                                 dimension_numbers=('NWC', 'WIO', 'NWC'),
                                 feature_group_count=CONV_CH)
    return y + bias


def mixer_layer(h, p, pos, ckv_past, krope_past, conv_past,
                ln_in, w_in, ln_q, w_q_b, ln_kv, w_kv_b, conv_w, conv_b,
                ln_conv_g, ln_conv_b, ln_out_a, ln_out_c, w_out, w_pe, ln_pg, w_pg):
    b, t = h.shape[0], h.shape[1]
    z = rmsnorm(h, ln_in) @ w_in
    i1 = Q_LORA
    i2 = i1 + KV_LORA
    i3 = i2 + QK_ROPE
    i4 = i3 + WIDTH_A
    i5 = i4 + CONV_CH
    i6 = i5 + CONV_CH
    c_q, c_kv, k_r, g_a, u_val, u_gate, g_c = jnp.split(z, [i1, i2, i3, i4, i5, i6], axis=-1)

    q = (rmsnorm(c_q, ln_q) @ w_q_b).reshape(b, t, N_HEADS_A, QK_NOPE + QK_ROPE)
    q_nope = q[..., :QK_NOPE]
    q_rope = rope(q[..., QK_NOPE:], pos)
    ckv_new = rmsnorm(c_kv, ln_kv)
    kr_new = rope(k_r[:, :, None, :], pos)[:, :, 0, :]
    w_kv = w_kv_b.reshape(KV_LORA, N_HEADS_A, QK_NOPE + V_HEAD)
    w_uk = w_kv[..., :QK_NOPE]
    w_uv = w_kv[..., QK_NOPE:]
    q_lat = jnp.einsum('bthn,rhn->bthr', q_nope, w_uk)
    if ckv_past is None:
        o_lat = prompt_attention(q_lat, q_rope, ckv_new, kr_new)
    else:
        o_lat = mla_attend(q_lat, q_rope,
                           jnp.concatenate([ckv_past, ckv_new], axis=1),
                           jnp.concatenate([krope_past, kr_new], axis=1), None)
    o_a = jnp.einsum('bthr,rhv->bthv', o_lat, w_uv).reshape(b, t, WIDTH_A)
    y_a = rmsnorm(o_a * jax.nn.silu(g_a), ln_out_a)

    u = u_val * jax.nn.sigmoid(u_gate)
    if conv_past is None:
        past = jnp.zeros((b, CONV_STATE, CONV_CH), u.dtype)
    else:
        past = conv_past
    u_pad = jnp.concatenate([past, u], axis=1)
    v = causal_dwconv(u_pad, conv_w, conv_b)
    v = jax.nn.silu(layernorm(v, ln_conv_g, ln_conv_b))
    y_c = rmsnorm(v * jax.nn.silu(g_c), ln_out_c)
    conv_state = u_pad[:, -CONV_STATE:, :]

    h = h + jnp.concatenate([y_a, y_c], axis=-1) @ w_out
    gate = jax.nn.sigmoid(rmsnorm(h, ln_pg) @ w_pg)
    h = h + gate * (p @ w_pe)
    return h, ckv_new, kr_new, conv_state


def setup_inputs(seed: int = 0) -> dict:
    key = jax.random.key(seed)
    ks = jax.random.split(key, 24)
    f32 = jnp.float32

    def nrm(k, shape, scale=1.0):
        return jax.random.normal(k, shape, f32) * scale

    def gain(k, shape):
        return 1.0 + 0.05 * jax.random.normal(k, shape, f32)

    return {
        'x_prompt': nrm(ks[0], (BATCH, SEQ, D_MODEL)),
        'x_sample': nrm(ks[1], (DEC_BATCH, DEC_SEQ, D_MODEL)),
        'cache_ckv': nrm(ks[2], (DEPTH, DEC_BATCH, PAST_LEN, KV_LORA)),
        'cache_krope': nrm(ks[3], (DEPTH, DEC_BATCH, PAST_LEN, QK_ROPE)),
        'state_conv': nrm(ks[4], (DEPTH, DEC_BATCH, CONV_STATE, CONV_CH), 0.5),
        'p_prompt': nrm(ks[5], (DEPTH, BATCH, SEQ, D_PLE)),
        'p_sample': nrm(ks[6], (DEPTH, DEC_BATCH, DEC_SEQ, D_PLE)),
        'ln_in': gain(ks[7], (DEPTH, D_MODEL)),
        'w_in': nrm(ks[8], (DEPTH, D_MODEL, D_IN), D_MODEL ** -0.5),
        'ln_q': gain(ks[9], (DEPTH, Q_LORA)),
        'w_q_b': nrm(ks[10], (DEPTH, Q_LORA, N_HEADS_A * (QK_NOPE + QK_ROPE)), Q_LORA ** -0.5),
        'ln_kv': gain(ks[11], (DEPTH, KV_LORA)),
        'w_kv_b': nrm(ks[12], (DEPTH, KV_LORA, N_HEADS_A * (QK_NOPE + V_HEAD)), KV_LORA ** -0.5),
        'conv_w': nrm(ks[13], (DEPTH, CONV_W, CONV_CH), CONV_W ** -0.5),
        'conv_b': nrm(ks[14], (DEPTH, CONV_CH), 0.01),
        'ln_conv_g': gain(ks[15], (DEPTH, CONV_CH)),
        'ln_conv_b': nrm(ks[16], (DEPTH, CONV_CH), 0.01),
        'ln_out_a': gain(ks[17], (DEPTH, WIDTH_A)),
        'ln_out_c': gain(ks[18], (DEPTH, CONV_CH)),
        'w_out': nrm(ks[19], (DEPTH, D_MIX, D_MODEL), D_MIX ** -0.5),
        'w_pe': nrm(ks[20], (DEPTH, D_PLE, D_MODEL), D_PLE ** -0.5),
        'ln_pg': gain(ks[21], (DEPTH, D_MODEL)),
        'w_pg': nrm(ks[22], (DEPTH, D_MODEL, D_MODEL), D_MODEL ** -0.5),
        'ln_f': gain(ks[23], (D_MODEL,)),
    }


def reference(x_prompt, x_sample, cache_ckv, cache_krope, state_conv, p_prompt, p_sample,
              ln_in, w_in, ln_q, w_q_b, ln_kv, w_kv_b, conv_w, conv_b, ln_conv_g, ln_conv_b,
              ln_out_a, ln_out_c, w_out, w_pe, ln_pg, w_pg, ln_f):
    past_len = cache_ckv.shape[2]
    pos_p = jnp.arange(x_prompt.shape[1])
    pos_s = past_len + jnp.arange(x_sample.shape[1])
    h_p = x_prompt
    h_s = x_sample
    ckv_p, kr_p, cv_p, ckv_s, kr_s, cv_s = [], [], [], [], [], []
    for i in range(DEPTH):
        lw = (ln_in[i], w_in[i], ln_q[i], w_q_b[i], ln_kv[i], w_kv_b[i], conv_w[i], conv_b[i],
              ln_conv_g[i], ln_conv_b[i], ln_out_a[i], ln_out_c[i], w_out[i], w_pe[i],
              ln_pg[i], w_pg[i])
        h_p, a, b_, c = mixer_layer(h_p, p_prompt[i], pos_p, None, None, None, *lw)
        ckv_p.append(a)
        kr_p.append(b_)
        cv_p.append(c)
        h_s, a, b_, c = mixer_layer(h_s, p_sample[i], pos_s, cache_ckv[i], cache_krope[i],
                                    state_conv[i], *lw)
        ckv_s.append(a)
        kr_s.append(b_)
        cv_s.append(c)
    y_prompt = rmsnorm(h_p, ln_f)
    y_sample = rmsnorm(h_s, ln_f)
    return (y_prompt, y_sample,
            jnp.stack(ckv_p), jnp.stack(kr_p), jnp.stack(cv_p),
            jnp.stack(ckv_s), jnp.stack(kr_s), jnp.stack(cv_s))
```

```python
import functools
import math

import jax
import jax.numpy as jnp
from jax import lax
from jax.experimental import pallas as pl
from jax.experimental.pallas import tpu as pltpu

F32 = jnp.float32
BF16 = jnp.bfloat16

N_HEADS = 8
QK_NOPE = 64
QK_ROPE = 32
V_HEAD = 64
KV_LORA = 128
Q_LORA = 256
D_MODEL = 1024
CONV_CH = 512
WIDTH_A = N_HEADS * V_HEAD
CONV_W = 31
CONV_STATE = CONV_W - 1
CHUNK = 64
ROPE_THETA = 10000.0
EPS = 1e-6
NEG_INF = -1e30

LANES = 128
HIST = 32
HEAD_W = 2 * LANES
ONES_COL = KV_LORA + QK_ROPE
Q_SCALE = (QK_NOPE + QK_ROPE) ** -0.5 * math.log2(math.e)

C_Q, C_KV, C_KR, C_GA, C_UV, C_UG, C_GC, C_END = 0, 256, 384, 512, 1024, 1536, 2048, 2560

VMEM_LIMIT = 56 * 1024 * 1024


def _rms(x, g):
    return x * lax.rsqrt(jnp.mean(x * x, axis=-1, keepdims=True) + EPS) * g


def _silu(x):
    return x * jax.nn.sigmoid(x)


def _rope(x, cos, sin):
    return x * cos + pltpu.roll(x, 64, 1) * sin


def _fold_kernel(wq_ref, wuk_ref, o_ref):
    o_ref[0, 0] = jnp.dot(wq_ref[0, 0], wuk_ref[0, 0], precision=lax.Precision.HIGHEST,
                          preferred_element_type=F32)


def _fold_q_uk(wq_nope, wuk_t):
    depth, heads = wq_nope.shape[:2]
    return pl.pallas_call(
        _fold_kernel,
        grid=(depth, heads),
        in_specs=[pl.BlockSpec((1, 1, Q_LORA, QK_NOPE), lambda l, h: (l, h, 0, 0)),
                  pl.BlockSpec((1, 1, QK_NOPE, KV_LORA), lambda l, h: (l, h, 0, 0))],
        out_specs=pl.BlockSpec((1, 1, Q_LORA, KV_LORA), lambda l, h: (l, h, 0, 0)),
        out_shape=jax.ShapeDtypeStruct((depth, heads, Q_LORA, KV_LORA), F32),
        name="fold_q_uk",
    )(wq_nope, wuk_t)


def _proj_in_kernel(h_ref, hist_ref, cos_ref, sin_ref, ln_in_ref, w_in_ref, ln_q_ref, wq_ref, ln_kv_ref,
                    cw_ref, cb_ref, lcg_ref, lcb_ref, loc_ref,
                    q_ref, kv_ref, ckv_ref, kr_ref, ga_ref, yc_ref, tail_ref,
                    ubuf_ref, sh_ref, gcs_ref, *, nb, ts, rc):
    n = nb * ts
    x = h_ref[...].reshape(n, D_MODEL)
    xn = _rms(x, ln_in_ref[...]).astype(BF16)

    def seg(a, b):
        return jnp.dot(xn, w_in_ref[:, a:b], preferred_element_type=F32)

    cos = cos_ref[...]
    sin = sin_ref[...]

    cqn = _rms(seg(C_Q, C_KV), ln_q_ref[...]).astype(BF16)
    ql = jnp.dot(cqn, wq_ref[:, :N_HEADS * LANES], preferred_element_type=F32) * Q_SCALE
    qr = jnp.dot(cqn, wq_ref[:, N_HEADS * LANES:], preferred_element_type=F32)
    for hh in range(N_HEADS):
        sl = slice(hh * LANES, (hh + 1) * LANES)
        q_ref[:, hh, :, 0:LANES] = ql[:, sl].astype(BF16).reshape(nb, ts, LANES)
        ro = _rope(qr[:, sl], cos, sin) * Q_SCALE
        q_ref[:, hh, :, LANES:HEAD_W] = ro.astype(BF16).reshape(nb, ts, LANES)

    ckvn = _rms(seg(C_KV, C_KR), ln_kv_ref[...])
    ckv_ref[...] = ckvn.reshape(nb, ts, KV_LORA)
    kro = _rope(seg(C_KR, C_GA), cos, sin)
    kr_ref[...] = kro[:, :QK_ROPE].reshape(nb, ts, QK_ROPE)
    lane = lax.broadcasted_iota(jnp.int32, (n, LANES), 1)
    kv_ref[:, :, 0:LANES] = ckvn.astype(BF16).reshape(nb, ts, LANES)
    kv_ref[:, :, LANES:HEAD_W] = jnp.where(lane == ONES_COL - LANES, 1.0, kro).astype(BF16).reshape(nb, ts, LANES)

    ga_ref[...] = _silu(seg(C_GA, C_UV)).astype(BF16).reshape(nb, ts, WIDTH_A)
    gcs_ref[...] = _silu(seg(C_GC, C_END))

    u = seg(C_UV, C_UG) * jax.nn.sigmoid(seg(C_UG, C_GC))

    @pl.when(pl.program_id(1) == 0)
    def _():
        ubuf_ref[:, 0:HIST, :] = hist_ref[...]

    ubuf_ref[:, HIST:HIST + ts, :] = u.reshape(nb, ts, CONV_CH)

    cb = cb_ref[...]
    lcg = lcg_ref[...]
    lcb = lcb_ref[...]
    loc = loc_ref[...]
    for s in range(nb):
        for b in range(1, 8):
            sh_ref[b - 1] = ubuf_ref[s, pl.ds(b, ts + 24), :]

        def chunk(r, carry, s=s):
            r0 = pl.multiple_of(r * rc, rc)
            acc = jnp.broadcast_to(cb, (rc, CONV_CH))
            for o in range(HIST - CONV_STATE, HIST + 1):
                a, b = divmod(o, 8)
                if b == 0:
                    src = ubuf_ref[s, pl.ds(r0 + 8 * a, rc), :]
                else:
                    src = sh_ref[b - 1, pl.ds(r0 + 8 * a, rc), :]
                k = o - (HIST - CONV_STATE)
                acc = acc + cw_ref[k:k + 1, :] * src
            mu = jnp.mean(acc, axis=-1, keepdims=True)
            xc = acc - mu
            var = jnp.mean(xc * xc, axis=-1, keepdims=True)
            v = _silu(xc * lax.rsqrt(var + EPS) * lcg + lcb)
            y = _rms(v * gcs_ref[pl.ds(s * ts + r0, rc), :], loc)
            yc_ref[s, pl.ds(r0, rc), :] = y.astype(BF16)
            return carry

        lax.fori_loop(0, ts // rc, chunk, 0)

    tail = ubuf_ref[:, ts:ts + HIST, :]
    tail_ref[...] = tail
    ubuf_ref[:, 0:HIST, :] = tail


def _proj_in(h, hist, cos, sin, lw, *, ts):
    bsz, t_len, _ = h.shape
    nb = bsz if t_len == ts and bsz * ts <= 512 else 1
    nt = t_len // ts
    rc = min(32, ts)
    const2 = lambda b, t: (0, 0)
    row3 = lambda b, t: (b, t, 0)
    seq3 = lambda b, t: (b, 0, 0)
    in_specs = [
        pl.BlockSpec((nb, ts, D_MODEL), row3),
        pl.BlockSpec((nb, HIST, CONV_CH), seq3),
        pl.BlockSpec((nb * ts, LANES), lambda b, t: (t, 0)),
        pl.BlockSpec((nb * ts, LANES), lambda b, t: (t, 0)),
        pl.BlockSpec((1, D_MODEL), const2),
        pl.BlockSpec((D_MODEL, C_END), const2),
        pl.BlockSpec((1, Q_LORA), const2),
        pl.BlockSpec((Q_LORA, 2 * N_HEADS * LANES), const2),
        pl.BlockSpec((1, KV_LORA), const2),
        pl.BlockSpec((CONV_W, CONV_CH), const2),
        pl.BlockSpec((1, CONV_CH), const2),
        pl.BlockSpec((1, CONV_CH), const2),
        pl.BlockSpec((1, CONV_CH), const2),
        pl.BlockSpec((1, CONV_CH), const2),
    ]
    out_shape = (
        jax.ShapeDtypeStruct((bsz, N_HEADS, t_len, HEAD_W), BF16),
        jax.ShapeDtypeStruct((bsz, t_len, HEAD_W), BF16),
        jax.ShapeDtypeStruct((bsz, t_len, KV_LORA), F32),
        jax.ShapeDtypeStruct((bsz, t_len, QK_ROPE), F32),
        jax.ShapeDtypeStruct((bsz, t_len, WIDTH_A), BF16),
        jax.ShapeDtypeStruct((bsz, t_len, CONV_CH), BF16),
        jax.ShapeDtypeStruct((bsz, HIST, CONV_CH), F32),
    )
    out_specs = (
        pl.BlockSpec((nb, N_HEADS, ts, HEAD_W), lambda b, t: (b, 0, t, 0)),
        pl.BlockSpec((nb, ts, HEAD_W), row3),
        pl.BlockSpec((nb, ts, KV_LORA), row3),
        pl.BlockSpec((nb, ts, QK_ROPE), row3),
        pl.BlockSpec((nb, ts, WIDTH_A), row3),
        pl.BlockSpec((nb, ts, CONV_CH), row3),
        pl.BlockSpec((nb, HIST, CONV_CH), seq3),
    )
    return pl.pallas_call(
        functools.partial(_proj_in_kernel, nb=nb, ts=ts, rc=rc),
        grid=(bsz // nb, nt),
        in_specs=in_specs,
        out_specs=out_specs,
        out_shape=out_shape,
        scratch_shapes=[pltpu.VMEM((nb, ts + HIST, CONV_CH), F32),
                        pltpu.VMEM((7, ts + 24, CONV_CH), F32),
                        pltpu.VMEM((nb * ts, CONV_CH), F32)],
        compiler_params=pltpu.CompilerParams(dimension_semantics=("parallel", "arbitrary"),
                                             vmem_limit_bytes=VMEM_LIMIT),
        name="proj_in",
    )(h, hist, cos, sin, lw["ln_in"], lw["w_in"], lw["ln_q"], lw["wq"], lw["ln_kv"],
      lw["conv_w"], lw["conv_b"], lw["ln_conv_g"], lw["ln_conv_b"], lw["ln_out_c"])


def _attend_kernel(q_ref, kv_ref, ga_ref, wuv_ref, lna_ref, ya_ref, acc_ref, m_ref,
                   *, tq, tk, tl, causal, kv_len):
    rows = N_HEADS * tq
    q = q_ref[0].reshape(rows, HEAD_W)
    acc_ref[...] = jnp.zeros_like(acc_ref)
    m_ref[...] = jnp.full_like(m_ref, NEG_INF)

    if causal:
        q0 = pl.program_id(1) * tq
        n_full = q0 // tk
    else:
        q0 = 0
        n_full = kv_len // tk

    def update(k, s):
        m_prev = m_ref[...]
        m_new = jnp.maximum(m_prev, jnp.max(s, axis=1, keepdims=True))
        p = jnp.exp2(s - m_new)
        alpha = jnp.exp2(m_prev - m_new)
        acc_ref[...] = alpha * acc_ref[...] + jnp.dot(p.astype(BF16), k, preferred_element_type=F32)
        m_ref[...] = m_new

    def scores(k):
        return lax.dot_general(q, k, (((1,), (1,)), ((), ())), preferred_element_type=F32)

    def step(j, carry):
        k = kv_ref[0, pl.ds(pl.multiple_of(j * tk, tk), tk), :]
        update(k, scores(k))
        return carry

    lax.fori_loop(0, n_full, step, 0)

    k0 = n_full * tk
    if causal:
        k0 = pl.multiple_of(k0, tk)
    k = kv_ref[0, pl.ds(k0, tl), :]
    col = k0 + lax.broadcasted_iota(jnp.int32, (rows, tl), 1)
    if causal:
        tok = q0 + (lax.broadcasted_iota(jnp.int32, (rows, tl), 0) & (tq - 1))
        limit = (tok | (CHUNK - 1)) + 1
    else:
        limit = kv_len
    update(k, jnp.where(col < limit, scores(k), NEG_INF))

    acc = acc_ref[...]
    lane = lax.broadcasted_iota(jnp.int32, (rows, LANES), 1)
    l = jnp.sum(jnp.where(lane == ONES_COL - LANES, acc[:, LANES:HEAD_W], 0.0), axis=1, keepdims=True)
    o = (acc[:, 0:LANES] / l).astype(BF16)
    cols = []
    for j in range(N_HEADS // 2):
        pair = jnp.concatenate([o[(2 * j) * tq:(2 * j + 1) * tq], o[(2 * j + 1) * tq:(2 * j + 2) * tq]], axis=1)
        cols.append(jnp.dot(pair, wuv_ref[j], preferred_element_type=F32))
    o_a = jnp.concatenate(cols, axis=1)
    ya_ref[0] = _rms(o_a * ga_ref[0].astype(F32), lna_ref[...]).astype(BF16)


def _attend(q, kv, ga, wuv, ln_out_a, *, tq, tk, tl, causal, kv_len):
    bsz, _, t_len, _ = q.shape
    s_len = kv.shape[1]
    rows = N_HEADS * tq
    return pl.pallas_call(
        functools.partial(_attend_kernel, tq=tq, tk=tk, tl=tl, causal=causal, kv_len=kv_len),
        grid=(bsz, t_len // tq),
        in_specs=[pl.BlockSpec((1, N_HEADS, tq, HEAD_W), lambda b, i: (b, 0, i, 0)),
                  pl.BlockSpec((1, s_len, HEAD_W), lambda b, i: (b, 0, 0)),
                  pl.BlockSpec((1, tq, WIDTH_A), lambda b, i: (b, i, 0)),
                  pl.BlockSpec((N_HEADS // 2, HEAD_W, LANES), lambda b, i: (0, 0, 0)),
                  pl.BlockSpec((1, WIDTH_A), lambda b, i: (0, 0))],
        out_specs=pl.BlockSpec((1, tq, WIDTH_A), lambda b, i: (b, i, 0)),
        out_shape=jax.ShapeDtypeStruct((bsz, t_len, WIDTH_A), BF16),
        scratch_shapes=[pltpu.VMEM((rows, HEAD_W), F32), pltpu.VMEM((rows, 1), F32)],
        compiler_params=pltpu.CompilerParams(dimension_semantics=("parallel", "arbitrary"),
                                             vmem_limit_bytes=VMEM_LIMIT),
        name="attend",
    )(q, kv, ga, wuv, ln_out_a)


def _proj_out_kernel(ya_ref, yc_ref, h_ref, p_ref, wo_ref, lnpg_ref, wpg_ref, wpe_ref, lnf_ref, out_ref, *, final):
    h1 = (h_ref[...]
          + jnp.dot(ya_ref[...], wo_ref[0:WIDTH_A, :], preferred_element_type=F32)
          + jnp.dot(yc_ref[...], wo_ref[WIDTH_A:, :], preferred_element_type=F32))
    gate = jax.nn.sigmoid(jnp.dot(_rms(h1, lnpg_ref[...]).astype(BF16), wpg_ref[...], preferred_element_type=F32))
    pe = jnp.dot(p_ref[...].astype(BF16), wpe_ref[...], preferred_element_type=F32)
    h2 = h1 + gate * pe
    out_ref[...] = _rms(h2, lnf_ref[...]) if final else h2


def _proj_out(ya, yc, h, p, lw, ln_f, *, final, tm):
    n = h.shape[0]
    d_ple = p.shape[1]
    row = lambda i: (i, 0)
    const = lambda i: (0, 0)
    return pl.pallas_call(
        functools.partial(_proj_out_kernel, final=final),
        grid=(n // tm,),
        in_specs=[pl.BlockSpec((tm, WIDTH_A), row),
                  pl.BlockSpec((tm, CONV_CH), row),
                  pl.BlockSpec((tm, D_MODEL), row),
                  pl.BlockSpec((tm, d_ple), row),
                  pl.BlockSpec((WIDTH_A + CONV_CH, D_MODEL), const),
                  pl.BlockSpec((1, D_MODEL), const),
                  pl.BlockSpec((D_MODEL, D_MODEL), const),
                  pl.BlockSpec((d_ple, D_MODEL), const),
                  pl.BlockSpec((1, D_MODEL), const)],
        out_specs=pl.BlockSpec((tm, D_MODEL), row),
        out_shape=jax.ShapeDtypeStruct((n, D_MODEL), F32),
        compiler_params=pltpu.CompilerParams(dimension_semantics=("parallel",),
                                             vmem_limit_bytes=VMEM_LIMIT),
        name="proj_out",
    )(ya, yc, h, p, lw["w_out"], lw["ln_pg"], lw["w_pg"], lw["w_pe"], ln_f)


def _rope_tables(pos):
    half = QK_ROPE // 2
    inv = ROPE_THETA ** (-jnp.arange(half, dtype=F32) / half)
    ang = pos.astype(F32)[:, None] * inv[None, :]
    cos, sin = jnp.cos(ang), jnp.sin(ang)
    pad = jnp.zeros((pos.shape[0], LANES - QK_ROPE), F32)
    return (jnp.concatenate([cos, cos, pad], axis=1), jnp.concatenate([-sin, sin, pad], axis=1))


def _rope_cols(w):
    half = QK_ROPE // 2
    z = jnp.zeros(w.shape[:-1] + (QK_ROPE,), w.dtype)
    return jnp.concatenate([w, z, w[..., half:], w[..., :half], z], axis=-1)


def _prep_layers(ln_in, w_in, ln_q, w_q_b, ln_kv, w_kv_b, conv_w, conv_b, ln_conv_g, ln_conv_b,
                 ln_out_a, ln_out_c, w_out, w_pe, ln_pg, w_pg):
    depth = w_in.shape[0]
    i1, i2, i3 = Q_LORA, Q_LORA + KV_LORA, Q_LORA + KV_LORA + QK_ROPE
    w_in_ext = jnp.concatenate([w_in[:, :, :i2], _rope_cols(w_in[:, :, i2:i3]), w_in[:, :, i3:]], axis=-1)

    wq4 = w_q_b.reshape(depth, Q_LORA, N_HEADS, QK_NOPE + QK_ROPE)
    wkv4 = w_kv_b.reshape(depth, KV_LORA, N_HEADS, QK_NOPE + V_HEAD)
    wq_nope = wq4[..., :QK_NOPE].transpose(0, 2, 1, 3)
    wuk_t = wkv4[..., :QK_NOPE].transpose(0, 2, 3, 1)
    w_fold = _fold_q_uk(wq_nope, wuk_t).transpose(0, 2, 1, 3).reshape(depth, Q_LORA, N_HEADS * KV_LORA)
    w_qr = _rope_cols(wq4[..., QK_NOPE:]).reshape(depth, Q_LORA, N_HEADS * LANES)
    wq = jnp.concatenate([w_fold, w_qr], axis=-1)

    w_uv = wkv4[..., QK_NOPE:]
    z = jnp.zeros((depth, KV_LORA, N_HEADS // 2, V_HEAD), w_uv.dtype)
    top = jnp.concatenate([w_uv[:, :, 0::2], z], axis=-1)
    bot = jnp.concatenate([z, w_uv[:, :, 1::2]], axis=-1)
    wuv = jnp.concatenate([top, bot], axis=1).transpose(0, 2, 1, 3)

    row = lambda a: a[:, None, :]
    return [dict(ln_in=row(ln_in)[l], w_in=w_in_ext[l].astype(BF16), ln_q=row(ln_q)[l], wq=wq[l].astype(BF16),
                 ln_kv=row(ln_kv)[l], conv_w=conv_w[l], conv_b=row(conv_b)[l], ln_conv_g=row(ln_conv_g)[l],
                 ln_conv_b=row(ln_conv_b)[l], ln_out_c=row(ln_out_c)[l], ln_out_a=row(ln_out_a)[l],
                 wuv=wuv[l].astype(BF16), w_out=w_out[l].astype(BF16), ln_pg=row(ln_pg)[l],
                 w_pg=w_pg[l].astype(BF16), w_pe=w_pe[l].astype(BF16))
            for l in range(depth)]


def _pick(n, cands):
    for c in cands:
        if n % c == 0:
            return c
    return n


def kernel(x_prompt, x_sample, cache_ckv, cache_krope, state_conv, p_prompt, p_sample, ln_in, w_in, ln_q, w_q_b, ln_kv, w_kv_b, conv_w, conv_b, ln_conv_g, ln_conv_b, ln_out_a, ln_out_c, w_out, w_pe, ln_pg, w_pg, ln_f):
    depth = w_in.shape[0]
    bp, tp, _ = x_prompt.shape
    bs, tsmp, _ = x_sample.shape
    past = cache_ckv.shape[2]
    layers = _prep_layers(ln_in, w_in, ln_q, w_q_b, ln_kv, w_kv_b, conv_w, conv_b, ln_conv_g, ln_conv_b,
                          ln_out_a, ln_out_c, w_out, w_pe, ln_pg, w_pg)
    ln_f2 = ln_f[None, :]

    ts_p = _pick(tp, (512, 256, 128))
    tq_p = _pick(tp, (128,))
    tk_p = _pick(tp, (512, 256, 128))
    tm_p = _pick(bp * tp, (512, 256, 128))
    cos_p, sin_p = _rope_tables(jnp.arange(tp))
    cos_s, sin_s = _rope_tables(past + jnp.arange(tsmp))
    cos_s, sin_s = jnp.tile(cos_s, (bs, 1)), jnp.tile(sin_s, (bs, 1))
    hist_p = jnp.zeros((bp, HIST, CONV_CH), F32)
    tk_s = _pick(past, (512, 256, 128))
    tl_s = -(-tsmp // LANES) * LANES
    ones_col = (jnp.arange(HEAD_W - KV_LORA - QK_ROPE) == 0).astype(F32)

    h_p = x_prompt
    h_s = x_sample
    outs = [[] for _ in range(6)]
    for l in range(depth):
        lw = layers[l]
        final = l == depth - 1
        q, kv, ckv, kr, ga, yc, tail = _proj_in(h_p, hist_p, cos_p, sin_p, lw, ts=ts_p)
        ya = _attend(q, kv, ga, lw["wuv"], lw["ln_out_a"], tq=tq_p, tk=tk_p, tl=tk_p, causal=True, kv_len=tp)
        h_p = _proj_out(ya.reshape(bp * tp, -1), yc.reshape(bp * tp, -1), h_p.reshape(bp * tp, -1),
                        p_prompt[l].reshape(bp * tp, -1), lw, ln_f2, final=final, tm=tm_p).reshape(bp, tp, -1)
        outs[0].append(ckv)
        outs[1].append(kr)
        outs[2].append(tail[:, HIST - CONV_STATE:, :])
        hist_s = jnp.pad(state_conv[l], ((0, 0), (HIST - CONV_STATE, 0), (0, 0)))
        q, kv, ckv, kr, ga, yc, tail = _proj_in(h_s, hist_s, cos_s, sin_s, lw, ts=tsmp)
        kv_past = jnp.concatenate(
            [cache_ckv[l], cache_krope[l], jnp.broadcast_to(ones_col, (bs, past, ones_col.shape[0]))], axis=-1)
        kv_all = jnp.concatenate(
            [kv_past.astype(BF16), kv, jnp.zeros((bs, tl_s - tsmp, HEAD_W), BF16)], axis=1)
        ya = _attend(q, kv_all, ga, lw["wuv"], lw["ln_out_a"], tq=tsmp, tk=tk_s, tl=tl_s, causal=False,
                     kv_len=past + tsmp)
        h_s = _proj_out(ya.reshape(bs * tsmp, -1), yc.reshape(bs * tsmp, -1), h_s.reshape(bs * tsmp, -1),
                        p_sample[l].reshape(bs * tsmp, -1), lw, ln_f2, final=final,
                        tm=bs * tsmp).reshape(bs, tsmp, -1)
        outs[3].append(ckv)
        outs[4].append(kr)
        outs[5].append(tail[:, HIST - CONV_STATE:, :])
    return (h_p, h_s) + tuple(jnp.stack(o) for o in outs)
```

```python
import functools
import math

import jax
import jax.numpy as jnp
from jax import lax
from jax.experimental import pallas as pl
from jax.experimental.pallas import tpu as pltpu

F32 = jnp.float32
BF16 = jnp.bfloat16

N_HEADS = 8
QK_NOPE = 64
QK_ROPE = 32
V_HEAD = 64
KV_LORA = 128
Q_LORA = 256
D_MODEL = 1024
CONV_CH = 512
WIDTH_A = N_HEADS * V_HEAD
CONV_W = 31
CONV_STATE = CONV_W - 1
CHUNK = 64
ROPE_THETA = 10000.0
EPS = 1e-6
NEG_INF = -1e30

LANES = 128
SUBLANES = 8
HIST = 32
HEAD_W = 2 * LANES
ONES_COL = KV_LORA + QK_ROPE
Q_SCALE = (QK_NOPE + QK_ROPE) ** -0.5 * math.log2(math.e)

C_Q, C_KV, C_KR, C_GA, C_UV, C_UG, C_GC, C_END = 0, 256, 384, 512, 1024, 1536, 2048, 2560

VMEM_LIMIT = 56 * 1024 * 1024
ROW_BLOCK = 128


def _rms(x, g):
    return x * lax.rsqrt(jnp.mean(x * x, axis=-1, keepdims=True) + EPS) * g


def _silu(x):
    return x * jax.nn.sigmoid(x)


def _rope(x, cos, sin):
    return x * cos + pltpu.roll(x, 64, 1) * sin


def _fold_kernel(wq_ref, wuk_ref, o_ref):
    o_ref[0, 0] = jnp.dot(wq_ref[0, 0], wuk_ref[0, 0], precision=lax.Precision.HIGHEST,
                          preferred_element_type=F32)


def _fold_q_uk(wq_nope, wuk_t):
    depth, heads = wq_nope.shape[:2]
    return pl.pallas_call(
        _fold_kernel,
        grid=(depth, heads),
        in_specs=[pl.BlockSpec((1, 1, Q_LORA, QK_NOPE), lambda l, h: (l, h, 0, 0)),
                  pl.BlockSpec((1, 1, QK_NOPE, KV_LORA), lambda l, h: (l, h, 0, 0))],
        out_specs=pl.BlockSpec((1, 1, Q_LORA, KV_LORA), lambda l, h: (l, h, 0, 0)),
        out_shape=jax.ShapeDtypeStruct((depth, heads, Q_LORA, KV_LORA), F32),
        name="fold_q_uk",
    )(wq_nope, wuk_t)


def _proj_in_kernel(h_ref, hist_ref, cos_ref, sin_ref, ln_in_ref, w_in_ref, ln_q_ref, wq_ref, ln_kv_ref,
                    cw_ref, cb_ref, lcg_ref, lcb_ref, loc_ref,
                    q_ref, kv_ref, ckv_ref, kr_ref, ga_ref, yc_ref, tail_ref,
                    ubuf_ref, sh_ref, gcs_ref, v_ref, wb_ref, *, nb, ts, rc):
    n = nb * ts
    x = h_ref[...].reshape(n, D_MODEL)
    xn = _rms(x, ln_in_ref[...]).astype(BF16)

    def seg(a, b):
        return jnp.dot(xn, w_in_ref[:, a:b], preferred_element_type=F32)

    cos = cos_ref[...]
    sin = sin_ref[...]

    cqn = _rms(seg(C_Q, C_KV), ln_q_ref[...]).astype(BF16)
    ql = jnp.dot(cqn, wq_ref[:, :N_HEADS * LANES], preferred_element_type=F32) * Q_SCALE
    qr = jnp.dot(cqn, wq_ref[:, N_HEADS * LANES:], preferred_element_type=F32)
    for hh in range(N_HEADS):
        sl = slice(hh * LANES, (hh + 1) * LANES)
        q_ref[:, hh, :, 0:LANES] = ql[:, sl].astype(BF16).reshape(nb, ts, LANES)
        ro = _rope(qr[:, sl], cos, sin) * Q_SCALE
        q_ref[:, hh, :, LANES:HEAD_W] = ro.astype(BF16).reshape(nb, ts, LANES)

    ckvn = _rms(seg(C_KV, C_KR), ln_kv_ref[...])
    ckv_ref[...] = ckvn.reshape(nb, ts, KV_LORA)
    kro = _rope(seg(C_KR, C_GA), cos, sin)
    kr_ref[...] = kro[:, :QK_ROPE].reshape(nb, ts, QK_ROPE)
    lane = lax.broadcasted_iota(jnp.int32, (n, LANES), 1)
    kv_ref[:, :, 0:LANES] = ckvn.astype(BF16).reshape(nb, ts, LANES)
    kv_ref[:, :, LANES:HEAD_W] = jnp.where(lane == ONES_COL - LANES, 1.0, kro).astype(BF16).reshape(nb, ts, LANES)

    ga_ref[...] = _silu(seg(C_GA, C_UV)).astype(BF16).reshape(nb, ts, WIDTH_A)
    gcs_ref[...] = _silu(seg(C_GC, C_END))

    u = seg(C_UV, C_UG) * jax.nn.sigmoid(seg(C_UG, C_GC))

    @pl.when(pl.program_id(1) == 0)
    def _():
        ubuf_ref[:, 0:HIST, :] = hist_ref[...]

    ubuf_ref[:, HIST:HIST + ts, :] = u.reshape(nb, ts, CONV_CH)

    for k in range(CONV_W):
        wb_ref[SUBLANES * k:SUBLANES * (k + 1), :] = jnp.broadcast_to(cw_ref[k:k + 1, :], (SUBLANES, CONV_CH))
    cb = jnp.broadcast_to(cb_ref[...], (SUBLANES, CONV_CH))
    ng = rc // SUBLANES
    for s in range(nb):
        for b in range(1, 8):
            sh_ref[b - 1] = ubuf_ref[s, pl.ds(b, ts + 24), :]

        def chunk(r, carry, s=s):
            r0 = pl.multiple_of(r * rc, rc)
            accs = [cb] * ng
            for o in range(HIST - CONV_STATE, HIST + 1):
                a, b = divmod(o, 8)
                k = o - (HIST - CONV_STATE)
                w8 = wb_ref[SUBLANES * k:SUBLANES * (k + 1), :]
                for g in range(ng):
                    row = pl.ds(r0 + SUBLANES * (a + g), SUBLANES)
                    src = ubuf_ref[s, row, :] if b == 0 else sh_ref[b - 1, row, :]
                    accs[g] = accs[g] + w8 * src
            for g in range(ng):
                v_ref[pl.ds(s * ts + r0 + SUBLANES * g, SUBLANES), :] = accs[g]
            return carry

        lax.fori_loop(0, ts // rc, chunk, 0)

    v = v_ref[...]
    xc = v - jnp.mean(v, axis=-1, keepdims=True)
    var = jnp.mean(xc * xc, axis=-1, keepdims=True)
    v = _silu(xc * lax.rsqrt(var + EPS) * lcg_ref[...] + lcb_ref[...])
    yc_ref[...] = _rms(v * gcs_ref[...], loc_ref[...]).astype(BF16).reshape(nb, ts, CONV_CH)

    tail = ubuf_ref[:, ts:ts + HIST, :]
    tail_ref[...] = tail
    ubuf_ref[:, 0:HIST, :] = tail


def _proj_in(h, hist, cos, sin, lw, *, ts):
    bsz, t_len, _ = h.shape
    nb = bsz if t_len == ts and bsz * ts <= 512 else 1
    nt = t_len // ts
    rc = min(32, ts)
    const2 = lambda b, t: (0, 0)
    row3 = lambda b, t: (b, t, 0)
    seq3 = lambda b, t: (b, 0, 0)
    in_specs = [
        pl.BlockSpec((nb, ts, D_MODEL), row3),
        pl.BlockSpec((nb, HIST, CONV_CH), seq3),
        pl.BlockSpec((nb * ts, LANES), lambda b, t: (t, 0)),
        pl.BlockSpec((nb * ts, LANES), lambda b, t: (t, 0)),
        pl.BlockSpec((1, D_MODEL), const2),
        pl.BlockSpec((D_MODEL, C_END), const2),
        pl.BlockSpec((1, Q_LORA), const2),
        pl.BlockSpec((Q_LORA, 2 * N_HEADS * LANES), const2),
        pl.BlockSpec((1, KV_LORA), const2),
        pl.BlockSpec((CONV_W, CONV_CH), const2),
        pl.BlockSpec((1, CONV_CH), const2),
        pl.BlockSpec((1, CONV_CH), const2),
        pl.BlockSpec((1, CONV_CH), const2),
        pl.BlockSpec((1, CONV_CH), const2),
    ]
    out_shape = (
        jax.ShapeDtypeStruct((bsz, N_HEADS, t_len, HEAD_W), BF16),
        jax.ShapeDtypeStruct((bsz, t_len, HEAD_W), BF16),
        jax.ShapeDtypeStruct((bsz, t_len, KV_LORA), F32),
        jax.ShapeDtypeStruct((bsz, t_len, QK_ROPE), F32),
        jax.ShapeDtypeStruct((bsz, t_len, WIDTH_A), BF16),
        jax.ShapeDtypeStruct((bsz, t_len, CONV_CH), BF16),
        jax.ShapeDtypeStruct((bsz, HIST, CONV_CH), F32),
    )
    out_specs = (
        pl.BlockSpec((nb, N_HEADS, ts, HEAD_W), lambda b, t: (b, 0, t, 0)),
        pl.BlockSpec((nb, ts, HEAD_W), row3),
        pl.BlockSpec((nb, ts, KV_LORA), row3),
        pl.BlockSpec((nb, ts, QK_ROPE), row3),
        pl.BlockSpec((nb, ts, WIDTH_A), row3),
        pl.BlockSpec((nb, ts, CONV_CH), row3),
        pl.BlockSpec((nb, HIST, CONV_CH), seq3),
    )
    return pl.pallas_call(
        functools.partial(_proj_in_kernel, nb=nb, ts=ts, rc=rc),
        grid=(bsz // nb, nt),
        in_specs=in_specs,
        out_specs=out_specs,
        out_shape=out_shape,
        scratch_shapes=[pltpu.VMEM((nb, ts + HIST, CONV_CH), F32),
                        pltpu.VMEM((7, ts + 24, CONV_CH), F32),
                        pltpu.VMEM((nb * ts, CONV_CH), F32),
                        pltpu.VMEM((nb * ts, CONV_CH), F32),
                        pltpu.VMEM((CONV_W * SUBLANES, CONV_CH), F32)],
        compiler_params=pltpu.CompilerParams(dimension_semantics=("parallel", "arbitrary"),
                                             vmem_limit_bytes=VMEM_LIMIT),
        name="proj_in",
    )(h, hist, cos, sin, lw["ln_in"], lw["w_in"], lw["ln_q"], lw["wq"], lw["ln_kv"],
      lw["conv_w"], lw["conv_b"], lw["ln_conv_g"], lw["ln_conv_b"], lw["ln_out_c"])


def _attend_kernel(q_ref, kv_ref, ga_ref, wuv_ref, lna_ref, ya_ref, acc_ref, m_ref, sa_ref, sb_ref,
                   *, tq, tk, tl, causal, kv_len):
    rows = N_HEADS * tq
    rb = min(ROW_BLOCK, rows)
    q = q_ref[0].reshape(rows, HEAD_W)
    acc_ref[...] = jnp.zeros_like(acc_ref)
    m_ref[...] = jnp.full_like(m_ref, NEG_INF)
    q0 = pl.program_id(1) * tq if causal else 0

    def key_tile(j, width):
        k0 = j * tk
        if not isinstance(k0, int):
            k0 = pl.multiple_of(k0, tk)
        return kv_ref[0, pl.ds(k0, width), :]

    def qk(j, s_ref, width):
        s_ref[:, 0:width] = lax.dot_general(q, key_tile(j, width), (((1,), (1,)), ((), ())),
                                            preferred_element_type=F32)

    def softmax_pv(s_ref, j, width, masked):
        nc = width // LANES
        if masked:
            col = j * tk + lax.broadcasted_iota(jnp.int32, (rb, width), 1)
        ps, alphas = [], []
        for r in range(rows // rb):
            sl = slice(r * rb, (r + 1) * rb)
            s = s_ref[sl, 0:width]
            if masked:
                if causal:
                    tok = q0 + ((r * rb + lax.broadcasted_iota(jnp.int32, (rb, width), 0)) & (tq - 1))
                    limit = (tok | (CHUNK - 1)) + 1
                else:
                    limit = kv_len
                s = jnp.where(col < limit, s, NEG_INF)
            sc = [s[:, c * LANES:(c + 1) * LANES] for c in range(nc)]
            mx = functools.reduce(jnp.maximum, sc)
            m_prev = m_ref[sl, :]
            m_new = jnp.maximum(m_prev, jnp.max(mx, axis=1, keepdims=True))
            m_ref[sl, :] = m_new
            alphas.append(jnp.exp2(m_prev - m_new))
            ps.append(jnp.concatenate([jnp.exp2(x - m_new) for x in sc], axis=1).astype(BF16))
        pv = jnp.dot(jnp.concatenate(ps, axis=0), key_tile(j, width), preferred_element_type=F32)
        for r in range(rows // rb):
            sl = slice(r * rb, (r + 1) * rb)
            for c in range(HEAD_W // LANES):
                cl = slice(c * LANES, (c + 1) * LANES)
                acc_ref[sl, cl] = alphas[r] * acc_ref[sl, cl] + pv[sl, cl]

    if causal:
        n_full = q0 // tk
        assert tl == tk
        qk(0, sa_ref, tk)

        def pair(jj, carry):
            j = 2 * jj
            qk(j + 1, sb_ref, tk)
            softmax_pv(sa_ref, j, tk, False)
            qk(j + 2, sa_ref, tk)
            softmax_pv(sb_ref, j + 1, tk, False)
            return carry

        lax.fori_loop(0, n_full // 2, pair, 0)
        odd = (n_full & 1) == 1

        @pl.when(odd)
        def _():
            qk(n_full, sb_ref, tk)
            softmax_pv(sa_ref, n_full - 1, tk, False)
            softmax_pv(sb_ref, n_full, tk, True)

        @pl.when(jnp.logical_not(odd))
        def _():
            softmax_pv(sa_ref, n_full, tk, True)
    else:
        n_full = kv_len // tk
        bufs = (sa_ref, sb_ref)
        qk(0, sa_ref, tk if n_full else tl)
        for j in range(n_full):
            qk(j + 1, bufs[(j + 1) % 2], tk if j + 1 < n_full else tl)
            softmax_pv(bufs[j % 2], j, tk, False)
        softmax_pv(bufs[n_full % 2], n_full, tl, True)

    acc = acc_ref[...]
    lane = lax.broadcasted_iota(jnp.int32, (rows, LANES), 1)
    l = jnp.sum(jnp.where(lane == ONES_COL - LANES, acc[:, LANES:HEAD_W], 0.0), axis=1, keepdims=True)
    o = (acc[:, 0:LANES] / l).astype(BF16)
    cols = []
    for j in range(N_HEADS // 2):
        pair_lhs = jnp.concatenate([o[(2 * j) * tq:(2 * j + 1) * tq], o[(2 * j + 1) * tq:(2 * j + 2) * tq]], axis=1)
        cols.append(jnp.dot(pair_lhs, wuv_ref[j], preferred_element_type=F32))
    o_a = jnp.concatenate(cols, axis=1)
    ya_ref[0] = _rms(o_a * ga_ref[0].astype(F32), lna_ref[...]).astype(BF16)


def _attend(q, kv, ga, wuv, ln_out_a, *, tq, tk, tl, causal, kv_len):
    bsz, _, t_len, _ = q.shape
    s_len = kv.shape[1]
    rows = N_HEADS * tq
    return pl.pallas_call(
        functools.partial(_attend_kernel, tq=tq, tk=tk, tl=tl, causal=causal, kv_len=kv_len),
        grid=(bsz, t_len // tq),
        in_specs=[pl.BlockSpec((1, N_HEADS, tq, HEAD_W), lambda b, i: (b, 0, i, 0)),
                  pl.BlockSpec((1, s_len, HEAD_W), lambda b, i: (b, 0, 0)),
                  pl.BlockSpec((1, tq, WIDTH_A), lambda b, i: (b, i, 0)),
                  pl.BlockSpec((N_HEADS // 2, HEAD_W, LANES), lambda b, i: (0, 0, 0)),
                  pl.BlockSpec((1, WIDTH_A), lambda b, i: (0, 0))],
        out_specs=pl.BlockSpec((1, tq, WIDTH_A), lambda b, i: (b, i, 0)),
        out_shape=jax.ShapeDtypeStruct((bsz, t_len, WIDTH_A), BF16),
        scratch_shapes=[pltpu.VMEM((rows, HEAD_W), F32), pltpu.VMEM((rows, LANES), F32),
                        pltpu.VMEM((rows, tk), F32), pltpu.VMEM((rows, tk), F32)],
        compiler_params=pltpu.CompilerParams(dimension_semantics=("parallel", "arbitrary"),
                                             vmem_limit_bytes=VMEM_LIMIT),
        name="attend",
    )(q, kv, ga, wuv, ln_out_a)


def _proj_out_kernel(ya_ref, yc_ref, h_ref, p_ref, wo_ref, lnpg_ref, wpg_ref, wpe_ref, lnf_ref, out_ref, *, final):
    h1 = (h_ref[...]
          + jnp.dot(ya_ref[...], wo_ref[0:WIDTH_A, :], preferred_element_type=F32)
          + jnp.dot(yc_ref[...], wo_ref[WIDTH_A:, :], preferred_element_type=F32))
    gate = jax.nn.sigmoid(jnp.dot(_rms(h1, lnpg_ref[...]).astype(BF16), wpg_ref[...], preferred_element_type=F32))
    pe = jnp.dot(p_ref[...].astype(BF16), wpe_ref[...], preferred_element_type=F32)
    h2 = h1 + gate * pe
    out_ref[...] = _rms(h2, lnf_ref[...]) if final else h2


def _proj_out(ya, yc, h, p, lw, ln_f, *, final, tm):
    n = h.shape[0]
    d_ple = p.shape[1]
    row = lambda i: (i, 0)
    const = lambda i: (0, 0)
    return pl.pallas_call(
        functools.partial(_proj_out_kernel, final=final),
        grid=(n // tm,),
        in_specs=[pl.BlockSpec((tm, WIDTH_A), row),
                  pl.BlockSpec((tm, CONV_CH), row),
                  pl.BlockSpec((tm, D_MODEL), row),
                  pl.BlockSpec((tm, d_ple), row),
                  pl.BlockSpec((WIDTH_A + CONV_CH, D_MODEL), const),
                  pl.BlockSpec((1, D_MODEL), const),
                  pl.BlockSpec((D_MODEL, D_MODEL), const),
                  pl.BlockSpec((d_ple, D_MODEL), const),
                  pl.BlockSpec((1, D_MODEL), const)],
        out_specs=pl.BlockSpec((tm, D_MODEL), row),
        out_shape=jax.ShapeDtypeStruct((n, D_MODEL), F32),
        compiler_params=pltpu.CompilerParams(dimension_semantics=("parallel",),
                                             vmem_limit_bytes=VMEM_LIMIT),
        name="proj_out",
    )(ya, yc, h, p, lw["w_out"], lw["ln_pg"], lw["w_pg"], lw["w_pe"], ln_f)


def _rope_tables(pos):
    half = QK_ROPE // 2
    inv = ROPE_THETA ** (-jnp.arange(half, dtype=F32) / half)
    ang = pos.astype(F32)[:, None] * inv[None, :]
    cos, sin = jnp.cos(ang), jnp.sin(ang)
    pad = jnp.zeros((pos.shape[0], LANES - QK_ROPE), F32)
    return (jnp.concatenate([cos, cos, pad], axis=1), jnp.concatenate([-sin, sin, pad], axis=1))


def _rope_cols(w):
    half = QK_ROPE // 2
    z = jnp.zeros(w.shape[:-1] + (QK_ROPE,), w.dtype)
    return jnp.concatenate([w, z, w[..., half:], w[..., :half], z], axis=-1)


def _prep_layers(ln_in, w_in, ln_q, w_q_b, ln_kv, w_kv_b, conv_w, conv_b, ln_conv_g, ln_conv_b,
                 ln_out_a, ln_out_c, w_out, w_pe, ln_pg, w_pg):
    depth = w_in.shape[0]
    i1, i2, i3 = Q_LORA, Q_LORA + KV_LORA, Q_LORA + KV_LORA + QK_ROPE
    w_in_ext = jnp.concatenate([w_in[:, :, :i2], _rope_cols(w_in[:, :, i2:i3]), w_in[:, :, i3:]], axis=-1)

    wq4 = w_q_b.reshape(depth, Q_LORA, N_HEADS, QK_NOPE + QK_ROPE)
    wkv4 = w_kv_b.reshape(depth, KV_LORA, N_HEADS, QK_NOPE + V_HEAD)
    wq_nope = wq4[..., :QK_NOPE].transpose(0, 2, 1, 3)
    wuk_t = wkv4[..., :QK_NOPE].transpose(0, 2, 3, 1)
    w_fold = _fold_q_uk(wq_nope, wuk_t).transpose(0, 2, 1, 3).reshape(depth, Q_LORA, N_HEADS * KV_LORA)
    w_qr = _rope_cols(wq4[..., QK_NOPE:]).reshape(depth, Q_LORA, N_HEADS * LANES)
    wq = jnp.concatenate([w_fold, w_qr], axis=-1)

    w_uv = wkv4[..., QK_NOPE:]
    z = jnp.zeros((depth, KV_LORA, N_HEADS // 2, V_HEAD), w_uv.dtype)
    top = jnp.concatenate([w_uv[:, :, 0::2], z], axis=-1)
    bot = jnp.concatenate([z, w_uv[:, :, 1::2]], axis=-1)
    wuv = jnp.concatenate([top, bot], axis=1).transpose(0, 2, 1, 3)

    row = lambda a: a[:, None, :]
    return [dict(ln_in=row(ln_in)[l], w_in=w_in_ext[l].astype(BF16), ln_q=row(ln_q)[l], wq=wq[l].astype(BF16),
                 ln_kv=row(ln_kv)[l], conv_w=conv_w[l], conv_b=row(conv_b)[l], ln_conv_g=row(ln_conv_g)[l],
                 ln_conv_b=row(ln_conv_b)[l], ln_out_c=row(ln_out_c)[l], ln_out_a=row(ln_out_a)[l],
                 wuv=wuv[l].astype(BF16), w_out=w_out[l].astype(BF16), ln_pg=row(ln_pg)[l],
                 w_pg=w_pg[l].astype(BF16), w_pe=w_pe[l].astype(BF16))
            for l in range(depth)]


def _pick(n, cands):
    for c in cands:
        if n % c == 0:
            return c
    return n


def kernel(x_prompt, x_sample, cache_ckv, cache_krope, state_conv, p_prompt, p_sample, ln_in, w_in, ln_q, w_q_b, ln_kv, w_kv_b, conv_w, conv_b, ln_conv_g, ln_conv_b, ln_out_a, ln_out_c, w_out, w_pe, ln_pg, w_pg, ln_f):
    depth = w_in.shape[0]
    bp, tp, _ = x_prompt.shape
    bs, tsmp, _ = x_sample.shape
    past = cache_ckv.shape[2]
    layers = _prep_layers(ln_in, w_in, ln_q, w_q_b, ln_kv, w_kv_b, conv_w, conv_b, ln_conv_g, ln_conv_b,
                          ln_out_a, ln_out_c, w_out, w_pe, ln_pg, w_pg)
    ln_f2 = ln_f[None, :]

    ts_p = _pick(tp, (512, 256, 128))
    tq_p = _pick(tp, (512, 256, 128))
    tk_p = _pick(tp, (512, 256, 128))
    tm_p = _pick(bp * tp, (512, 256, 128))
    cos_p, sin_p = _rope_tables(jnp.arange(tp))
    cos_s, sin_s = _rope_tables(past + jnp.arange(tsmp))
    cos_s, sin_s = jnp.tile(cos_s, (bs, 1)), jnp.tile(sin_s, (bs, 1))
    hist_p = jnp.zeros((bp, HIST, CONV_CH), F32)
    tk_s = _pick(past, (512, 256, 128))
    tl_s = -(-tsmp // LANES) * LANES
    ones_col = (jnp.arange(HEAD_W - KV_LORA - QK_ROPE) == 0).astype(F32)

    h_p = x_prompt
    h_s = x_sample
    outs = [[] for _ in range(6)]
    for l in range(depth):
        lw = layers[l]
        final = l == depth - 1
        q, kv, ckv, kr, ga, yc, tail = _proj_in(h_p, hist_p, cos_p, sin_p, lw, ts=ts_p)
        ya = _attend(q, kv, ga, lw["wuv"], lw["ln_out_a"], tq=tq_p, tk=tk_p, tl=tk_p, causal=True, kv_len=tp)
        h_p = _proj_out(ya.reshape(bp * tp, -1), yc.reshape(bp * tp, -1), h_p.reshape(bp * tp, -1),
                        p_prompt[l].reshape(bp * tp, -1), lw, ln_f2, final=final, tm=tm_p).reshape(bp, tp, -1)
        outs[0].append(ckv)
        outs[1].append(kr)
        outs[2].append(tail[:, HIST - CONV_STATE:, :])
        hist_s = jnp.pad(state_conv[l], ((0, 0), (HIST - CONV_STATE, 0), (0, 0)))
        q, kv, ckv, kr, ga, yc, tail = _proj_in(h_s, hist_s, cos_s, sin_s, lw, ts=tsmp)
        kv_past = jnp.concatenate(
            [cache_ckv[l], cache_krope[l], jnp.broadcast_to(ones_col, (bs, past, ones_col.shape[0]))], axis=-1)
        kv_all = jnp.concatenate(
            [kv_past.astype(BF16), kv, jnp.zeros((bs, tl_s - tsmp, HEAD_W), BF16)], axis=1)
        ya = _attend(q, kv_all, ga, lw["wuv"], lw["ln_out_a"], tq=tsmp, tk=tk_s, tl=tl_s, causal=False,
                     kv_len=past + tsmp)
        h_s = _proj_out(ya.reshape(bs * tsmp, -1), yc.reshape(bs * tsmp, -1), h_s.reshape(bs * tsmp, -1),
                        p_sample[l].reshape(bs * tsmp, -1), lw, ln_f2, final=final,
                        tm=bs * tsmp).reshape(bs, tsmp, -1)
        outs[3].append(ckv)
        outs[4].append(kr)
        outs[5].append(tail[:, HIST - CONV_STATE:, :])
    return (h_p, h_s) + tuple(jnp.stack(o) for o in outs)
```

```python
import functools
import math

import jax
import jax.numpy as jnp
from jax import lax
from jax.experimental import pallas as pl
from jax.experimental.pallas import tpu as pltpu

F32 = jnp.float32
BF16 = jnp.bfloat16

N_HEADS = 8
QK_NOPE = 64
QK_ROPE = 32
V_HEAD = 64
KV_LORA = 128
Q_LORA = 256
D_MODEL = 1024
CONV_CH = 512
WIDTH_A = N_HEADS * V_HEAD
CONV_W = 31
CONV_STATE = CONV_W - 1
CHUNK = 64
ROPE_THETA = 10000.0
EPS = 1e-6
NEG_INF = -1e30

LANES = 128
SUBLANES = 8
HIST = 32
HEAD_W = 2 * LANES
ONES_COL = KV_LORA + QK_ROPE
Q_SCALE = (QK_NOPE + QK_ROPE) ** -0.5 * math.log2(math.e)

C_Q, C_KV, C_KR, C_GA, C_UV, C_UG, C_GC, C_END = 0, 256, 384, 512, 1024, 1536, 2048, 2560

VMEM_LIMIT = 56 * 1024 * 1024
ROW_BLOCK = 128


def _rms(x, g):
    return x * lax.rsqrt(jnp.mean(x * x, axis=-1, keepdims=True) + EPS) * g


def _silu(x):
    return x * jax.nn.sigmoid(x)


def _rope(x, cos, sin):
    return x * cos + pltpu.roll(x, 64, 1) * sin


def _fold_kernel(wq_ref, wuk_ref, o_ref):
    o_ref[0, 0] = jnp.dot(wq_ref[0, 0], wuk_ref[0, 0], precision=lax.Precision.HIGHEST,
                          preferred_element_type=F32)


def _fold_q_uk(wq_nope, wuk_t):
    depth, heads = wq_nope.shape[:2]
    return pl.pallas_call(
        _fold_kernel,
        grid=(depth, heads),
        in_specs=[pl.BlockSpec((1, 1, Q_LORA, QK_NOPE), lambda l, h: (l, h, 0, 0)),
                  pl.BlockSpec((1, 1, QK_NOPE, KV_LORA), lambda l, h: (l, h, 0, 0))],
        out_specs=pl.BlockSpec((1, 1, Q_LORA, KV_LORA), lambda l, h: (l, h, 0, 0)),
        out_shape=jax.ShapeDtypeStruct((depth, heads, Q_LORA, KV_LORA), F32),
        name="fold_q_uk",
    )(wq_nope, wuk_t)


def _proj_in_kernel(h_ref, hist_ref, cos_ref, sin_ref, ln_in_ref, w_in_ref, ln_q_ref, wq_ref, ln_kv_ref,
                    cw_ref, cb_ref, lcg_ref, lcb_ref, loc_ref,
                    q_ref, kv_ref, ckv_ref, kr_ref, ga_ref, yc_ref, tail_ref,
                    ubuf_ref, sh_ref, gcs_ref, v_ref, wb_ref, *, nb, ts, rc):
    n = nb * ts

    @pl.when(pl.program_id(1) == 0)
    def _():
        ubuf_ref[:, 0:HIST, :] = hist_ref[...]

    x = h_ref[...].reshape(n, D_MODEL)
    xn = _rms(x, ln_in_ref[...]).astype(BF16)

    def seg(a, b):
        return jnp.dot(xn, w_in_ref[:, a:b], preferred_element_type=F32)

    u = seg(C_UV, C_UG) * jax.nn.sigmoid(seg(C_UG, C_GC))
    ubuf_ref[:, HIST:HIST + ts, :] = u.reshape(nb, ts, CONV_CH)

    cqn_box = []

    def piece_cq():
        cqn_box.append(_rms(seg(C_Q, C_KV), ln_q_ref[...]).astype(BF16))

    def piece_q(hh):
        qh = jnp.dot(cqn_box[0], wq_ref[:, hh * HEAD_W:(hh + 1) * HEAD_W], preferred_element_type=F32)
        q_ref[:, hh, :, 0:LANES] = (qh[:, 0:LANES] * Q_SCALE).astype(BF16).reshape(nb, ts, LANES)
        ro = _rope(qh[:, LANES:HEAD_W], cos_ref[...], sin_ref[...]) * Q_SCALE
        q_ref[:, hh, :, LANES:HEAD_W] = ro.astype(BF16).reshape(nb, ts, LANES)

    def piece_kv():
        z = seg(C_KV, C_GA)
        ckvn = _rms(z[:, 0:KV_LORA], ln_kv_ref[...])
        ckv_ref[...] = ckvn.reshape(nb, ts, KV_LORA)
        kro = _rope(z[:, KV_LORA:], cos_ref[...], sin_ref[...])
        kr_ref[...] = kro[:, :QK_ROPE].reshape(nb, ts, QK_ROPE)
        lane = lax.broadcasted_iota(jnp.int32, (n, LANES), 1)
        kv_ref[:, :, 0:LANES] = ckvn.astype(BF16).reshape(nb, ts, LANES)
        kv_ref[:, :, LANES:HEAD_W] = jnp.where(lane == ONES_COL - LANES, 1.0, kro).astype(BF16).reshape(nb, ts, LANES)

    def piece_ga(j):
        cl = slice(j * HEAD_W, (j + 1) * HEAD_W)
        ga_ref[:, :, cl] = _silu(seg(C_GA + cl.start, C_GA + cl.stop)).astype(BF16).reshape(nb, ts, HEAD_W)

    def piece_gc(j):
        cl = slice(j * HEAD_W, (j + 1) * HEAD_W)
        gcs_ref[:, cl] = _silu(seg(C_GC + cl.start, C_GC + cl.stop))

    pieces = ([piece_cq, piece_kv]
              + [functools.partial(piece_ga, j) for j in range(WIDTH_A // HEAD_W)]
              + [functools.partial(piece_gc, j) for j in range(CONV_CH // HEAD_W)]
              + [functools.partial(piece_q, hh) for hh in range(N_HEADS)])

    for k in range(CONV_W):
        wb_ref[SUBLANES * k:SUBLANES * (k + 1), :] = jnp.broadcast_to(cw_ref[k:k + 1, :], (SUBLANES, CONV_CH))
    cb = jnp.broadcast_to(cb_ref[...], (SUBLANES, CONV_CH))
    ng = rc // SUBLANES
    for s in range(nb):
        for b in range(1, 8):
            sh_ref[b - 1] = ubuf_ref[s, pl.ds(b, ts + 24), :]

        def chunk(r, carry, s=s):
            r0 = r * rc if isinstance(r, int) else pl.multiple_of(r * rc, rc)
            accs = [cb] * ng
            for o in range(HIST - CONV_STATE, HIST + 1):
                a, b = divmod(o, 8)
                k = o - (HIST - CONV_STATE)
                w8 = wb_ref[SUBLANES * k:SUBLANES * (k + 1), :]
                for g in range(ng):
                    row = pl.ds(r0 + SUBLANES * (a + g), SUBLANES)
                    src = ubuf_ref[s, row, :] if b == 0 else sh_ref[b - 1, row, :]
                    accs[g] = accs[g] + w8 * src
            for g in range(ng):
                v_ref[pl.ds(s * ts + r0 + SUBLANES * g, SUBLANES), :] = accs[g]
            return carry

        if nb == 1:
            for r in range(ts // rc):
                chunk(r, 0)
                if r < len(pieces):
                    pieces[r]()
            del pieces[:ts // rc]
        else:
            lax.fori_loop(0, ts // rc, chunk, 0)

    for piece in pieces:
        piece()

    v = v_ref[...]
    xc = v - jnp.mean(v, axis=-1, keepdims=True)
    var = jnp.mean(xc * xc, axis=-1, keepdims=True)
    v = _silu(xc * lax.rsqrt(var + EPS) * lcg_ref[...] + lcb_ref[...])
    yc_ref[...] = _rms(v * gcs_ref[...], loc_ref[...]).astype(BF16).reshape(nb, ts, CONV_CH)

    tail = ubuf_ref[:, ts:ts + HIST, :]
    tail_ref[...] = tail
    ubuf_ref[:, 0:HIST, :] = tail


def _proj_in(h, hist, cos, sin, lw, *, ts):
    bsz, t_len, _ = h.shape
    nb = bsz if t_len == ts and bsz * ts <= 512 else 1
    nt = t_len // ts
    rc = min(32, ts)
    const2 = lambda b, t: (0, 0)
    row3 = lambda b, t: (b, t, 0)
    seq3 = lambda b, t: (b, 0, 0)
    in_specs = [
        pl.BlockSpec((nb, ts, D_MODEL), row3),
        pl.BlockSpec((nb, HIST, CONV_CH), seq3),
        pl.BlockSpec((nb * ts, LANES), lambda b, t: (t, 0)),
        pl.BlockSpec((nb * ts, LANES), lambda b, t: (t, 0)),
        pl.BlockSpec((1, D_MODEL), const2),
        pl.BlockSpec((D_MODEL, C_END), const2),
        pl.BlockSpec((1, Q_LORA), const2),
        pl.BlockSpec((Q_LORA, 2 * N_HEADS * LANES), const2),
        pl.BlockSpec((1, KV_LORA), const2),
        pl.BlockSpec((CONV_W, CONV_CH), const2),
        pl.BlockSpec((1, CONV_CH), const2),
        pl.BlockSpec((1, CONV_CH), const2),
        pl.BlockSpec((1, CONV_CH), const2),
        pl.BlockSpec((1, CONV_CH), const2),
    ]
    out_shape = (
        jax.ShapeDtypeStruct((bsz, N_HEADS, t_len, HEAD_W), BF16),
        jax.ShapeDtypeStruct((bsz, t_len, HEAD_W), BF16),
        jax.ShapeDtypeStruct((bsz, t_len, KV_LORA), F32),
        jax.ShapeDtypeStruct((bsz, t_len, QK_ROPE), F32),
        jax.ShapeDtypeStruct((bsz, t_len, WIDTH_A), BF16),
        jax.ShapeDtypeStruct((bsz, t_len, CONV_CH), BF16),
        jax.ShapeDtypeStruct((bsz, HIST, CONV_CH), F32),
    )
    out_specs = (
        pl.BlockSpec((nb, N_HEADS, ts, HEAD_W), lambda b, t: (b, 0, t, 0)),
        pl.BlockSpec((nb, ts, HEAD_W), row3),
        pl.BlockSpec((nb, ts, KV_LORA), row3),
        pl.BlockSpec((nb, ts, QK_ROPE), row3),
        pl.BlockSpec((nb, ts, WIDTH_A), row3),
        pl.BlockSpec((nb, ts, CONV_CH), row3),
        pl.BlockSpec((nb, HIST, CONV_CH), seq3),
    )
    return pl.pallas_call(
        functools.partial(_proj_in_kernel, nb=nb, ts=ts, rc=rc),
        grid=(bsz // nb, nt),
        in_specs=in_specs,
        out_specs=out_specs,
        out_shape=out_shape,
        scratch_shapes=[pltpu.VMEM((nb, ts + HIST, CONV_CH), F32),
                        pltpu.VMEM((7, ts + 24, CONV_CH), F32),
                        pltpu.VMEM((nb * ts, CONV_CH), F32),
                        pltpu.VMEM((nb * ts, CONV_CH), F32),
                        pltpu.VMEM((CONV_W * SUBLANES, CONV_CH), F32)],
        compiler_params=pltpu.CompilerParams(dimension_semantics=("parallel", "arbitrary"),
                                             vmem_limit_bytes=VMEM_LIMIT),
        name="proj_in",
    )(h, hist, cos, sin, lw["ln_in"], lw["w_in"], lw["ln_q"], lw["wq"], lw["ln_kv"],
      lw["conv_w"], lw["conv_b"], lw["ln_conv_g"], lw["ln_conv_b"], lw["ln_out_c"])


def _attend_kernel(q_ref, kv_ref, ga_ref, wuv_ref, lna_ref, ya_ref, acc_ref, m_ref, sa_ref, sb_ref,
                   *, tq, tk, tl, causal, kv_len):
    rows = N_HEADS * tq
    rb = min(ROW_BLOCK, rows)
    q = q_ref[0].reshape(rows, HEAD_W)
    acc_ref[...] = jnp.zeros_like(acc_ref)
    m_ref[...] = jnp.full_like(m_ref, NEG_INF)

    def key_tile(j, width):
        k0 = j * tk
        if not isinstance(k0, int):
            k0 = pl.multiple_of(k0, tk)
        return kv_ref[0, pl.ds(k0, width), :]

    def qk(qv, j, s_ref, width):
        s_ref[:, 0:width] = lax.dot_general(qv, key_tile(j, width), (((1,), (1,)), ((), ())),
                                            preferred_element_type=F32)

    if causal:
        assert tq == tk == tl and rb == LANES and LANES == 2 * CHUNK and tq % rb == 0
        drow = lax.broadcasted_iota(jnp.int32, (rb, LANES), 0)
        dlane = lax.broadcasted_iota(jnp.int32, (rb, LANES), 1)
        diag_mask = dlane < (drow | (CHUNK - 1)) + 1
    else:
        tail_mask = lax.broadcasted_iota(jnp.int32, (rb, LANES), 1) < kv_len - (kv_len // tk) * tk

    def softmax_pv(s_ref, j, width, masked):
        nc = width // LANES
        ps, alphas = [], []
        for r in range(rows // rb):
            sl = slice(r * rb, (r + 1) * rb)
            nvis = ((r * rb) % tq) // LANES + 1 if (masked and causal) else nc
            sc = [s_ref[sl, c * LANES:(c + 1) * LANES] for c in range(nvis)]
            if masked:
                sc[-1] = jnp.where(diag_mask if causal else tail_mask, sc[-1], NEG_INF)
            mx = functools.reduce(jnp.maximum, sc)
            m_prev = m_ref[sl, :]
            m_new = jnp.maximum(m_prev, jnp.max(mx, axis=1, keepdims=True))
            m_ref[sl, :] = m_new
            alphas.append(jnp.exp2(m_prev - m_new))
            p = [jnp.exp2(x - m_new).astype(BF16) for x in sc] + [jnp.zeros((rb, LANES), BF16)] * (nc - nvis)
            ps.append(jnp.concatenate(p, axis=1))
        pv = jnp.dot(jnp.concatenate(ps, axis=0), key_tile(j, width), preferred_element_type=F32)
        for r in range(rows // rb):
            sl = slice(r * rb, (r + 1) * rb)
            for c in range(HEAD_W // LANES):
                cl = slice(c * LANES, (c + 1) * LANES)
                acc_ref[sl, cl] = alphas[r] * acc_ref[sl, cl] + pv[sl, cl]

    if causal:
        n = pl.program_id(1)
        qk(q, 0, sa_ref, tk)

        def pair(jj, carry):
            j = 2 * jj
            qk(q, j + 1, sb_ref, tk)
            softmax_pv(sa_ref, j, tk, False)
            qk(q, j + 2, sa_ref, tk)
            softmax_pv(sb_ref, j + 1, tk, False)
            return carry

        lax.fori_loop(0, n // 2, pair, 0)
        odd = (n & 1) == 1

        @pl.when(odd)
        def _():
            qk(q, n, sb_ref, tk)
            softmax_pv(sa_ref, n - 1, tk, False)
            softmax_pv(sb_ref, n, tk, True)

        @pl.when(jnp.logical_not(odd))
        def _():
            softmax_pv(sa_ref, n, tk, True)
    else:
        n_full = kv_len // tk
        bufs = (sa_ref, sb_ref)
        qk(q, 0, sa_ref, tk if n_full else tl)
        for j in range(n_full):
            qk(q, j + 1, bufs[(j + 1) % 2], tk if j + 1 < n_full else tl)
            softmax_pv(bufs[j % 2], j, tk, False)
        softmax_pv(bufs[n_full % 2], n_full, tl, True)

    acc = acc_ref[...]
    lane = lax.broadcasted_iota(jnp.int32, (rows, LANES), 1)
    l = jnp.sum(jnp.where(lane == ONES_COL - LANES, acc[:, LANES:HEAD_W], 0.0), axis=1, keepdims=True)
    o = (acc[:, 0:LANES] / l).astype(BF16)
    cols = []
    for j in range(N_HEADS // 2):
        pair_lhs = jnp.concatenate([o[(2 * j) * tq:(2 * j + 1) * tq], o[(2 * j + 1) * tq:(2 * j + 2) * tq]], axis=1)
        cols.append(jnp.dot(pair_lhs, wuv_ref[j], preferred_element_type=F32))
    o_a = jnp.concatenate(cols, axis=1)
    ya_ref[0] = _rms(o_a * ga_ref[0].astype(F32), lna_ref[...]).astype(BF16)


def _attend(q, kv, ga, wuv, ln_out_a, *, tq, tk, tl, causal, kv_len):
    bsz, _, t_len, _ = q.shape
    s_len = kv.shape[1]
    rows = N_HEADS * tq
    return pl.pallas_call(
        functools.partial(_attend_kernel, tq=tq, tk=tk, tl=tl, causal=causal, kv_len=kv_len),
        grid=(bsz, t_len // tq),
        in_specs=[pl.BlockSpec((1, N_HEADS, tq, HEAD_W), lambda b, i: (b, 0, i, 0)),
                  pl.BlockSpec((1, s_len, HEAD_W), lambda b, i: (b, 0, 0)),
                  pl.BlockSpec((1, tq, WIDTH_A), lambda b, i: (b, i, 0)),
                  pl.BlockSpec((N_HEADS // 2, HEAD_W, LANES), lambda b, i: (0, 0, 0)),
                  pl.BlockSpec((1, WIDTH_A), lambda b, i: (0, 0))],
        out_specs=pl.BlockSpec((1, tq, WIDTH_A), lambda b, i: (b, i, 0)),
        out_shape=jax.ShapeDtypeStruct((bsz, t_len, WIDTH_A), BF16),
        scratch_shapes=[pltpu.VMEM((rows, HEAD_W), F32), pltpu.VMEM((rows, LANES), F32),
                        pltpu.VMEM((rows, tk), F32), pltpu.VMEM((rows, tk), F32)],
        compiler_params=pltpu.CompilerParams(dimension_semantics=("parallel", "arbitrary"),
                                             vmem_limit_bytes=VMEM_LIMIT),
        name="attend",
    )(q, kv, ga, wuv, ln_out_a)


def _proj_out_kernel(ya_ref, yc_ref, h_ref, p_ref, wo_ref, lnpg_ref, wpg_ref, wpe_ref, lnf_ref, out_ref, *, final):
    h1 = (h_ref[...]
          + jnp.dot(ya_ref[...], wo_ref[0:WIDTH_A, :], preferred_element_type=F32)
          + jnp.dot(yc_ref[...], wo_ref[WIDTH_A:, :], preferred_element_type=F32))
    gate = jax.nn.sigmoid(jnp.dot(_rms(h1, lnpg_ref[...]).astype(BF16), wpg_ref[...], preferred_element_type=F32))
    pe = jnp.dot(p_ref[0].astype(BF16), wpe_ref[...], preferred_element_type=F32)
    h2 = h1 + gate * pe
    out_ref[...] = _rms(h2, lnf_ref[...]) if final else h2


def _proj_out(ya, yc, h, p, layer, lw, ln_f, *, final, tm):
    n = h.shape[0]
    d_ple = p.shape[2]
    row = lambda i: (i, 0)
    const = lambda i: (0, 0)
    return pl.pallas_call(
        functools.partial(_proj_out_kernel, final=final),
        grid=(n // tm,),
        in_specs=[pl.BlockSpec((tm, WIDTH_A), row),
                  pl.BlockSpec((tm, CONV_CH), row),
                  pl.BlockSpec((tm, D_MODEL), row),
                  pl.BlockSpec((1, tm, d_ple), lambda i: (layer, i, 0)),
                  pl.BlockSpec((WIDTH_A + CONV_CH, D_MODEL), const),
                  pl.BlockSpec((1, D_MODEL), const),
                  pl.BlockSpec((D_MODEL, D_MODEL), const),
                  pl.BlockSpec((d_ple, D_MODEL), const),
                  pl.BlockSpec((1, D_MODEL), const)],
        out_specs=pl.BlockSpec((tm, D_MODEL), row),
        out_shape=jax.ShapeDtypeStruct((n, D_MODEL), F32),
        compiler_params=pltpu.CompilerParams(dimension_semantics=("parallel",),
                                             vmem_limit_bytes=VMEM_LIMIT),
        name="proj_out",
    )(ya, yc, h, p, lw["w_out"], lw["ln_pg"], lw["w_pg"], lw["w_pe"], ln_f)


def _rope_tables(pos):
    half = QK_ROPE // 2
    inv = ROPE_THETA ** (-jnp.arange(half, dtype=F32) / half)
    ang = pos.astype(F32)[:, None] * inv[None, :]
    cos, sin = jnp.cos(ang), jnp.sin(ang)
    pad = jnp.zeros((pos.shape[0], LANES - QK_ROPE), F32)
    return (jnp.concatenate([cos, cos, pad], axis=1), jnp.concatenate([-sin, sin, pad], axis=1))


def _rope_cols(w):
    half = QK_ROPE // 2
    z = jnp.zeros(w.shape[:-1] + (QK_ROPE,), w.dtype)
    return jnp.concatenate([w, z, w[..., half:], w[..., :half], z], axis=-1)


def _prep_layers(ln_in, w_in, ln_q, w_q_b, ln_kv, w_kv_b, conv_w, conv_b, ln_conv_g, ln_conv_b,
                 ln_out_a, ln_out_c, w_out, w_pe, ln_pg, w_pg):
    depth = w_in.shape[0]
    i1, i2, i3 = Q_LORA, Q_LORA + KV_LORA, Q_LORA + KV_LORA + QK_ROPE
    w_in_ext = jnp.concatenate([w_in[:, :, :i2], _rope_cols(w_in[:, :, i2:i3]), w_in[:, :, i3:]], axis=-1)

    wq4 = w_q_b.reshape(depth, Q_LORA, N_HEADS, QK_NOPE + QK_ROPE)
    wkv4 = w_kv_b.reshape(depth, KV_LORA, N_HEADS, QK_NOPE + V_HEAD)
    wq_nope = wq4[..., :QK_NOPE].transpose(0, 2, 1, 3)
    wuk_t = wkv4[..., :QK_NOPE].transpose(0, 2, 3, 1)
    w_fold = _fold_q_uk(wq_nope, wuk_t).transpose(0, 2, 1, 3).reshape(depth, Q_LORA, N_HEADS * KV_LORA)
    w_qr = _rope_cols(wq4[..., QK_NOPE:]).reshape(depth, Q_LORA, N_HEADS * LANES)
    wq = jnp.stack([w_fold.reshape(depth, Q_LORA, N_HEADS, LANES), w_qr.reshape(depth, Q_LORA, N_HEADS, LANES)],
                   axis=3).reshape(depth, Q_LORA, N_HEADS * HEAD_W)

    w_uv = wkv4[..., QK_NOPE:]
    z = jnp.zeros((depth, KV_LORA, N_HEADS // 2, V_HEAD), w_uv.dtype)
    top = jnp.concatenate([w_uv[:, :, 0::2], z], axis=-1)
    bot = jnp.concatenate([z, w_uv[:, :, 1::2]], axis=-1)
    wuv = jnp.concatenate([top, bot], axis=1).transpose(0, 2, 1, 3)

    row = lambda a: a[:, None, :]
    return [dict(ln_in=row(ln_in)[l], w_in=w_in_ext[l].astype(BF16), ln_q=row(ln_q)[l], wq=wq[l].astype(BF16),
                 ln_kv=row(ln_kv)[l], conv_w=conv_w[l], conv_b=row(conv_b)[l], ln_conv_g=row(ln_conv_g)[l],
                 ln_conv_b=row(ln_conv_b)[l], ln_out_c=row(ln_out_c)[l], ln_out_a=row(ln_out_a)[l],
                 wuv=wuv[l].astype(BF16), w_out=w_out[l].astype(BF16), ln_pg=row(ln_pg)[l],
                 w_pg=w_pg[l].astype(BF16), w_pe=w_pe[l].astype(BF16))
            for l in range(depth)]


def _pick(n, cands):
    for c in cands:
        if n % c == 0:
            return c
    return n


def kernel(x_prompt, x_sample, cache_ckv, cache_krope, state_conv, p_prompt, p_sample, ln_in, w_in, ln_q, w_q_b, ln_kv, w_kv_b, conv_w, conv_b, ln_conv_g, ln_conv_b, ln_out_a, ln_out_c, w_out, w_pe, ln_pg, w_pg, ln_f):
    depth = w_in.shape[0]
    bp, tp, _ = x_prompt.shape
    bs, tsmp, _ = x_sample.shape
    past = cache_ckv.shape[2]
    layers = _prep_layers(ln_in, w_in, ln_q, w_q_b, ln_kv, w_kv_b, conv_w, conv_b, ln_conv_g, ln_conv_b,
                          ln_out_a, ln_out_c, w_out, w_pe, ln_pg, w_pg)
    ln_f2 = ln_f[None, :]

    ts_p = _pick(tp, (512, 256, 128))
    tq_p = _pick(tp, (512, 256, 128))
    tk_p = tq_p
    tm_p = _pick(bp * tp, (512, 256, 128))
    cos_p, sin_p = _rope_tables(jnp.arange(tp))
    cos_s, sin_s = _rope_tables(past + jnp.arange(tsmp))
    cos_s, sin_s = jnp.tile(cos_s, (bs, 1)), jnp.tile(sin_s, (bs, 1))
    hist_p = jnp.zeros((bp, HIST, CONV_CH), F32)
    tk_s = _pick(past, (512, 256, 128))
    tl_s = -(-tsmp // LANES) * LANES
    ones_col = (jnp.arange(HEAD_W - KV_LORA - QK_ROPE) == 0).astype(F32)

    h_p = x_prompt
    h_s = x_sample
    outs = [[] for _ in range(6)]
    for l in range(depth):
        lw = layers[l]
        final = l == depth - 1
        q, kv, ckv, kr, ga, yc, tail = _proj_in(h_p, hist_p, cos_p, sin_p, lw, ts=ts_p)
        ya = _attend(q, kv, ga, lw["wuv"], lw["ln_out_a"], tq=tq_p, tk=tk_p, tl=tk_p, causal=True, kv_len=tp)
        h_p = _proj_out(ya.reshape(bp * tp, -1), yc.reshape(bp * tp, -1), h_p.reshape(bp * tp, -1),
                        p_prompt.reshape(depth, bp * tp, -1), l, lw, ln_f2, final=final, tm=tm_p).reshape(bp, tp, -1)
        outs[0].append(ckv)
        outs[1].append(kr)
        outs[2].append(tail[:, HIST - CONV_STATE:, :])
        hist_s = jnp.pad(state_conv[l], ((0, 0), (HIST - CONV_STATE, 0), (0, 0)))
        q, kv, ckv, kr, ga, yc, tail = _proj_in(h_s, hist_s, cos_s, sin_s, lw, ts=tsmp)
        kv_past = jnp.concatenate(
            [cache_ckv[l], cache_krope[l], jnp.broadcast_to(ones_col, (bs, past, ones_col.shape[0]))], axis=-1)
        kv_all = jnp.concatenate(
            [kv_past.astype(BF16), kv, jnp.zeros((bs, tl_s - tsmp, HEAD_W), BF16)], axis=1)
        ya = _attend(q, kv_all, ga, lw["wuv"], lw["ln_out_a"], tq=tsmp, tk=tk_s, tl=tl_s, causal=False,
                     kv_len=past + tsmp)
        h_s = _proj_out(ya.reshape(bs * tsmp, -1), yc.reshape(bs * tsmp, -1), h_s.reshape(bs * tsmp, -1),
                        p_sample.reshape(depth, bs * tsmp, -1), l, lw, ln_f2, final=final,
                        tm=bs * tsmp).reshape(bs, tsmp, -1)
        outs[3].append(ckv)
        outs[4].append(kr)
        outs[5].append(tail[:, HIST - CONV_STATE:, :])
    return (h_p, h_s) + tuple(jnp.stack(o) for o in outs)
```

```python
import functools
import math

import jax
import jax.numpy as jnp
from jax import lax
from jax.experimental import pallas as pl
from jax.experimental.pallas import tpu as pltpu

F32 = jnp.float32
BF16 = jnp.bfloat16

N_HEADS = 8
QK_NOPE = 64
QK_ROPE = 32
V_HEAD = 64
KV_LORA = 128
Q_LORA = 256
D_MODEL = 1024
CONV_CH = 512
WIDTH_A = N_HEADS * V_HEAD
CONV_W = 31
CONV_STATE = CONV_W - 1
CHUNK = 64
ROPE_THETA = 10000.0
EPS = 1e-6
NEG_INF = -1e30

LANES = 128
SUBLANES = 8
HIST = 32
HEAD_W = 2 * LANES
ONES_COL = KV_LORA + QK_ROPE
Q_SCALE = (QK_NOPE + QK_ROPE) ** -0.5 * math.log2(math.e)

C_Q, C_KV, C_KR, C_GA, C_UV, C_UG, C_GC, C_END = 0, 256, 384, 512, 1024, 1536, 2048, 2560

VMEM_LIMIT = 56 * 1024 * 1024
ROW_BLOCK = 128


def _rms(x, g):
    return x * lax.rsqrt(jnp.mean(x * x, axis=-1, keepdims=True) + EPS) * g


def _silu(x):
    return x * jax.nn.sigmoid(x)


def _rope(x, cos, sin):
    return x * cos + pltpu.roll(x, 64, 1) * sin


def _fold_kernel(wq_ref, wuk_ref, o_ref):
    o_ref[0, 0] = jnp.dot(wq_ref[0, 0], wuk_ref[0, 0], precision=lax.Precision.HIGHEST,
                          preferred_element_type=F32)


def _fold_q_uk(wq_nope, wuk_t):
    depth, heads = wq_nope.shape[:2]
    return pl.pallas_call(
        _fold_kernel,
        grid=(depth, heads),
        in_specs=[pl.BlockSpec((1, 1, Q_LORA, QK_NOPE), lambda l, h: (l, h, 0, 0)),
                  pl.BlockSpec((1, 1, QK_NOPE, KV_LORA), lambda l, h: (l, h, 0, 0))],
        out_specs=pl.BlockSpec((1, 1, Q_LORA, KV_LORA), lambda l, h: (l, h, 0, 0)),
        out_shape=jax.ShapeDtypeStruct((depth, heads, Q_LORA, KV_LORA), F32),
        name="fold_q_uk",
    )(wq_nope, wuk_t)


def _proj_in_kernel(h_ref, hist_ref, cos_ref, sin_ref, ln_in_ref, w_in_ref, ln_q_ref, wq_ref, ln_kv_ref,
                    cw_ref, cb_ref, lcg_ref, lcb_ref, loc_ref,
                    q_ref, kv_ref, ckv_ref, kr_ref, ga_ref, yc_ref, tail_ref,
                    ubuf_ref, sh_ref, gcs_ref, v_ref, wb_ref, *, nb, ts, rc):
    n = nb * ts

    @pl.when(pl.program_id(1) == 0)
    def _():
        ubuf_ref[:, 0:HIST, :] = hist_ref[...]

    x = h_ref[...].reshape(n, D_MODEL)
    xn = _rms(x, ln_in_ref[...]).astype(BF16)

    def seg(a, b):
        return jnp.dot(xn, w_in_ref[:, a:b], preferred_element_type=F32)

    u = seg(C_UV, C_UG) * jax.nn.sigmoid(seg(C_UG, C_GC))
    ubuf_ref[:, HIST:HIST + ts, :] = u.reshape(nb, ts, CONV_CH)

    cqn_box = []

    def cq_finish(z):
        cqn_box.append(_rms(z, ln_q_ref[...]).astype(BF16))

    def q_start(hh):
        return jnp.dot(cqn_box[0], wq_ref[:, hh * HEAD_W:(hh + 1) * HEAD_W], preferred_element_type=F32)

    def q_finish(hh, qh):
        q_ref[:, hh, :, 0:LANES] = (qh[:, 0:LANES] * Q_SCALE).astype(BF16).reshape(nb, ts, LANES)
        ro = _rope(qh[:, LANES:HEAD_W], cos_ref[...], sin_ref[...]) * Q_SCALE
        q_ref[:, hh, :, LANES:HEAD_W] = ro.astype(BF16).reshape(nb, ts, LANES)

    def kv_finish(z):
        ckvn = _rms(z[:, 0:KV_LORA], ln_kv_ref[...])
        ckv_ref[...] = ckvn.reshape(nb, ts, KV_LORA)
        kro = _rope(z[:, KV_LORA:], cos_ref[...], sin_ref[...])
        kr_ref[...] = kro[:, :QK_ROPE].reshape(nb, ts, QK_ROPE)
        lane = lax.broadcasted_iota(jnp.int32, (n, LANES), 1)
        kv_ref[:, :, 0:LANES] = ckvn.astype(BF16).reshape(nb, ts, LANES)
        kv_ref[:, :, LANES:HEAD_W] = jnp.where(lane == ONES_COL - LANES, 1.0, kro).astype(BF16).reshape(nb, ts, LANES)

    def ga_finish(j, z):
        ga_ref[:, :, j * HEAD_W:(j + 1) * HEAD_W] = _silu(z).astype(BF16).reshape(nb, ts, HEAD_W)

    def gc_finish(j, z):
        gcs_ref[:, j * HEAD_W:(j + 1) * HEAD_W] = _silu(z)

    part = functools.partial
    pieces = ([(part(seg, C_Q, C_KV), cq_finish), (part(seg, C_KV, C_GA), kv_finish)]
              + [(part(seg, C_GA + j * HEAD_W, C_GA + (j + 1) * HEAD_W), part(ga_finish, j))
                 for j in range(WIDTH_A // HEAD_W)]
              + [(part(seg, C_GC + j * HEAD_W, C_GC + (j + 1) * HEAD_W), part(gc_finish, j))
                 for j in range(CONV_CH // HEAD_W)]
              + [(part(q_start, hh), part(q_finish, hh)) for hh in range(N_HEADS)])

    for k in range(CONV_W):
        wb_ref[SUBLANES * k:SUBLANES * (k + 1), :] = jnp.broadcast_to(cw_ref[k:k + 1, :], (SUBLANES, CONV_CH))
    cb = jnp.broadcast_to(cb_ref[...], (SUBLANES, CONV_CH))
    ng = rc // SUBLANES
    for s in range(nb):
        for b in range(1, 8):
            sh_ref[b - 1] = ubuf_ref[s, pl.ds(b, ts + 24), :]

        def chunk(r, carry, s=s):
            r0 = r * rc if isinstance(r, int) else pl.multiple_of(r * rc, rc)
            accs = [cb] * ng
            for o in range(HIST - CONV_STATE, HIST + 1):
                a, b = divmod(o, 8)
                k = o - (HIST - CONV_STATE)
                w8 = wb_ref[SUBLANES * k:SUBLANES * (k + 1), :]
                for g in range(ng):
                    row = pl.ds(r0 + SUBLANES * (a + g), SUBLANES)
                    src = ubuf_ref[s, row, :] if b == 0 else sh_ref[b - 1, row, :]
                    accs[g] = accs[g] + w8 * src
            for g in range(ng):
                v_ref[pl.ds(s * ts + r0 + SUBLANES * g, SUBLANES), :] = accs[g]
            return carry

        if nb == 1:
            for r in range(ts // rc):
                z = pieces[r][0]() if r < len(pieces) else None
                chunk(r, 0)
                if r < len(pieces):
                    pieces[r][1](z)
            del pieces[:ts // rc]
        else:
            lax.fori_loop(0, ts // rc, chunk, 0)

    for start, finish in pieces:
        finish(start())

    v = v_ref[...]
    xc = v - jnp.mean(v, axis=-1, keepdims=True)
    var = jnp.mean(xc * xc, axis=-1, keepdims=True)
    v = _silu(xc * lax.rsqrt(var + EPS) * lcg_ref[...] + lcb_ref[...])
    yc_ref[...] = _rms(v * gcs_ref[...], loc_ref[...]).astype(BF16).reshape(nb, ts, CONV_CH)

    tail = ubuf_ref[:, ts:ts + HIST, :]
    tail_ref[...] = tail
    ubuf_ref[:, 0:HIST, :] = tail


def _proj_in(h, hist, cos, sin, lw, *, ts):
    bsz, t_len, _ = h.shape
    nb = bsz if t_len == ts and bsz * ts <= 512 else 1
    nt = t_len // ts
    rc = min(32, ts)
    const2 = lambda b, t: (0, 0)
    row3 = lambda b, t: (b, t, 0)
    seq3 = lambda b, t: (b, 0, 0)
    in_specs = [
        pl.BlockSpec((nb, ts, D_MODEL), row3),
        pl.BlockSpec((nb, HIST, CONV_CH), seq3),
        pl.BlockSpec((nb * ts, LANES), lambda b, t: (t, 0)),
        pl.BlockSpec((nb * ts, LANES), lambda b, t: (t, 0)),
        pl.BlockSpec((1, D_MODEL), const2),
        pl.BlockSpec((D_MODEL, C_END), const2),
        pl.BlockSpec((1, Q_LORA), const2),
        pl.BlockSpec((Q_LORA, 2 * N_HEADS * LANES), const2),
        pl.BlockSpec((1, KV_LORA), const2),
        pl.BlockSpec((CONV_W, CONV_CH), const2),
        pl.BlockSpec((1, CONV_CH), const2),
        pl.BlockSpec((1, CONV_CH), const2),
        pl.BlockSpec((1, CONV_CH), const2),
        pl.BlockSpec((1, CONV_CH), const2),
    ]
    out_shape = (
        jax.ShapeDtypeStruct((bsz, N_HEADS, t_len, HEAD_W), BF16),
        jax.ShapeDtypeStruct((bsz, t_len, HEAD_W), BF16),
        jax.ShapeDtypeStruct((bsz, t_len, KV_LORA), F32),
        jax.ShapeDtypeStruct((bsz, t_len, QK_ROPE), F32),
        jax.ShapeDtypeStruct((bsz, t_len, WIDTH_A), BF16),
        jax.ShapeDtypeStruct((bsz, t_len, CONV_CH), BF16),
        jax.ShapeDtypeStruct((bsz, HIST, CONV_CH), F32),
    )
    out_specs = (
        pl.BlockSpec((nb, N_HEADS, ts, HEAD_W), lambda b, t: (b, 0, t, 0)),
        pl.BlockSpec((nb, ts, HEAD_W), row3),
        pl.BlockSpec((nb, ts, KV_LORA), row3),
        pl.BlockSpec((nb, ts, QK_ROPE), row3),
        pl.BlockSpec((nb, ts, WIDTH_A), row3),
        pl.BlockSpec((nb, ts, CONV_CH), row3),
        pl.BlockSpec((nb, HIST, CONV_CH), seq3),
    )
    return pl.pallas_call(
        functools.partial(_proj_in_kernel, nb=nb, ts=ts, rc=rc),
        grid=(bsz // nb, nt),
        in_specs=in_specs,
        out_specs=out_specs,
        out_shape=out_shape,
        scratch_shapes=[pltpu.VMEM((nb, ts + HIST, CONV_CH), F32),
                        pltpu.VMEM((7, ts + 24, CONV_CH), F32),
                        pltpu.VMEM((nb * ts, CONV_CH), F32),
                        pltpu.VMEM((nb * ts, CONV_CH), F32),
                        pltpu.VMEM((CONV_W * SUBLANES, CONV_CH), F32)],
        compiler_params=pltpu.CompilerParams(dimension_semantics=("parallel", "arbitrary"),
                                             vmem_limit_bytes=VMEM_LIMIT),
        name="proj_in",
    )(h, hist, cos, sin, lw["ln_in"], lw["w_in"], lw["ln_q"], lw["wq"], lw["ln_kv"],
      lw["conv_w"], lw["conv_b"], lw["ln_conv_g"], lw["ln_conv_b"], lw["ln_out_c"])


def _attend_kernel(q_ref, kv_ref, ga_ref, wuv_ref, lna_ref, ya_ref, acc_ref, m_ref, sa_ref, sb_ref,
                   *, tq, tk, tl, causal, kv_len):
    rows = N_HEADS * tq
    rb = min(ROW_BLOCK, rows)
    q = q_ref[0].reshape(rows, HEAD_W)
    acc_ref[...] = jnp.zeros_like(acc_ref)
    m_ref[...] = jnp.full_like(m_ref, NEG_INF)

    def key_tile(j, width):
        k0 = j * tk
        if not isinstance(k0, int):
            k0 = pl.multiple_of(k0, tk)
        return kv_ref[0, pl.ds(k0, width), :]

    def qk(qv, j, s_ref, width):
        s_ref[:, 0:width] = lax.dot_general(qv, key_tile(j, width), (((1,), (1,)), ((), ())),
                                            preferred_element_type=F32)

    if causal:
        assert tq == tk == tl and rb == LANES and LANES == 2 * CHUNK and tq % rb == 0
        drow = lax.broadcasted_iota(jnp.int32, (rb, LANES), 0)
        dlane = lax.broadcasted_iota(jnp.int32, (rb, LANES), 1)
        diag_mask = dlane < (drow | (CHUNK - 1)) + 1
    else:
        tail_mask = lax.broadcasted_iota(jnp.int32, (rb, LANES), 1) < kv_len - (kv_len // tk) * tk

    def softmax_pv(s_ref, j, width, masked):
        nc = width // LANES
        ps, alphas = [], []
        for r in range(rows // rb):
            sl = slice(r * rb, (r + 1) * rb)
            nvis = ((r * rb) % tq) // LANES + 1 if (masked and causal) else nc
            sc = [s_ref[sl, c * LANES:(c + 1) * LANES] for c in range(nvis)]
            if masked:
                sc[-1] = jnp.where(diag_mask if causal else tail_mask, sc[-1], NEG_INF)
            mx = functools.reduce(jnp.maximum, sc)
            m_prev = m_ref[sl, :]
            m_new = jnp.maximum(m_prev, jnp.max(mx, axis=1, keepdims=True))
            m_ref[sl, :] = m_new
            alphas.append(jnp.exp2(m_prev - m_new))
            p = [jnp.exp2(x - m_new).astype(BF16) for x in sc] + [jnp.zeros((rb, LANES), BF16)] * (nc - nvis)
            ps.append(jnp.concatenate(p, axis=1))
        pv = jnp.dot(jnp.concatenate(ps, axis=0), key_tile(j, width), preferred_element_type=F32)
        for r in range(rows // rb):
            sl = slice(r * rb, (r + 1) * rb)
            for c in range(HEAD_W // LANES):
                cl = slice(c * LANES, (c + 1) * LANES)
                acc_ref[sl, cl] = alphas[r] * acc_ref[sl, cl] + pv[sl, cl]

    if causal:
        n = pl.program_id(1)
        qk(q, 0, sa_ref, tk)

        def pair(jj, carry):
            j = 2 * jj
            qk(q, j + 1, sb_ref, tk)
            softmax_pv(sa_ref, j, tk, False)
            qk(q, j + 2, sa_ref, tk)
            softmax_pv(sb_ref, j + 1, tk, False)
            return carry

        lax.fori_loop(0, n // 2, pair, 0)
        odd = (n & 1) == 1

        @pl.when(odd)
        def _():
            qk(q, n, sb_ref, tk)
            softmax_pv(sa_ref, n - 1, tk, False)
            softmax_pv(sb_ref, n, tk, True)

        @pl.when(jnp.logical_not(odd))
        def _():
            softmax_pv(sa_ref, n, tk, True)
    else:
        n_full = kv_len // tk
        bufs = (sa_ref, sb_ref)
        qk(q, 0, sa_ref, tk if n_full else tl)
        for j in range(n_full):
            qk(q, j + 1, bufs[(j + 1) % 2], tk if j + 1 < n_full else tl)
            softmax_pv(bufs[j % 2], j, tk, False)
        softmax_pv(bufs[n_full % 2], n_full, tl, True)

    acc = acc_ref[...]
    lane = lax.broadcasted_iota(jnp.int32, (rows, LANES), 1)
    l = jnp.sum(jnp.where(lane == ONES_COL - LANES, acc[:, LANES:HEAD_W], 0.0), axis=1, keepdims=True)
    o = (acc[:, 0:LANES] / l).astype(BF16)
    cols = []
    for j in range(N_HEADS // 2):
        pair_lhs = jnp.concatenate([o[(2 * j) * tq:(2 * j + 1) * tq], o[(2 * j + 1) * tq:(2 * j + 2) * tq]], axis=1)
        cols.append(jnp.dot(pair_lhs, wuv_ref[j], preferred_element_type=F32))
    o_a = jnp.concatenate(cols, axis=1)
    ya_ref[0] = _rms(o_a * ga_ref[0].astype(F32), lna_ref[...]).astype(BF16)


def _attend(q, kv, ga, wuv, ln_out_a, *, tq, tk, tl, causal, kv_len):
    bsz, _, t_len, _ = q.shape
    s_len = kv.shape[1]
    rows = N_HEADS * tq
    return pl.pallas_call(
        functools.partial(_attend_kernel, tq=tq, tk=tk, tl=tl, causal=causal, kv_len=kv_len),
        grid=(bsz, t_len // tq),
        in_specs=[pl.BlockSpec((1, N_HEADS, tq, HEAD_W), lambda b, i: (b, 0, i, 0)),
                  pl.BlockSpec((1, s_len, HEAD_W), lambda b, i: (b, 0, 0)),
                  pl.BlockSpec((1, tq, WIDTH_A), lambda b, i: (b, i, 0)),
                  pl.BlockSpec((N_HEADS // 2, HEAD_W, LANES), lambda b, i: (0, 0, 0)),
                  pl.BlockSpec((1, WIDTH_A), lambda b, i: (0, 0))],
        out_specs=pl.BlockSpec((1, tq, WIDTH_A), lambda b, i: (b, i, 0)),
        out_shape=jax.ShapeDtypeStruct((bsz, t_len, WIDTH_A), BF16),
        scratch_shapes=[pltpu.VMEM((rows, HEAD_W), F32), pltpu.VMEM((rows, LANES), F32),
                        pltpu.VMEM((rows, tk), F32), pltpu.VMEM((rows, tk), F32)],
        compiler_params=pltpu.CompilerParams(dimension_semantics=("parallel", "arbitrary"),
                                             vmem_limit_bytes=VMEM_LIMIT),
        name="attend",
    )(q, kv, ga, wuv, ln_out_a)


def _proj_out_kernel(ya_ref, yc_ref, h_ref, p_ref, wo_ref, lnpg_ref, wpg_ref, wpe_ref, lnf_ref, out_ref, *, final):
    h1 = (h_ref[...]
          + jnp.dot(ya_ref[...], wo_ref[0:WIDTH_A, :], preferred_element_type=F32)
          + jnp.dot(yc_ref[...], wo_ref[WIDTH_A:, :], preferred_element_type=F32))
    gate = jax.nn.sigmoid(jnp.dot(_rms(h1, lnpg_ref[...]).astype(BF16), wpg_ref[...], preferred_element_type=F32))
    pe = jnp.dot(p_ref[0].astype(BF16), wpe_ref[...], preferred_element_type=F32)
    h2 = h1 + gate * pe
    out_ref[...] = _rms(h2, lnf_ref[...]) if final else h2


def _proj_out(ya, yc, h, p, layer, lw, ln_f, *, final, tm):
    n = h.shape[0]
    d_ple = p.shape[2]
    row = lambda i: (i, 0)
    const = lambda i: (0, 0)
    return pl.pallas_call(
        functools.partial(_proj_out_kernel, final=final),
        grid=(n // tm,),
        in_specs=[pl.BlockSpec((tm, WIDTH_A), row),
                  pl.BlockSpec((tm, CONV_CH), row),
                  pl.BlockSpec((tm, D_MODEL), row),
                  pl.BlockSpec((1, tm, d_ple), lambda i: (layer, i, 0)),
                  pl.BlockSpec((WIDTH_A + CONV_CH, D_MODEL), const),
                  pl.BlockSpec((1, D_MODEL), const),
                  pl.BlockSpec((D_MODEL, D_MODEL), const),
                  pl.BlockSpec((d_ple, D_MODEL), const),
                  pl.BlockSpec((1, D_MODEL), const)],
        out_specs=pl.BlockSpec((tm, D_MODEL), row),
        out_shape=jax.ShapeDtypeStruct((n, D_MODEL), F32),
        compiler_params=pltpu.CompilerParams(dimension_semantics=("parallel",),
                                             vmem_limit_bytes=VMEM_LIMIT),
        name="proj_out",
    )(ya, yc, h, p, lw["w_out"], lw["ln_pg"], lw["w_pg"], lw["w_pe"], ln_f)


def _rope_tables(pos):
    half = QK_ROPE // 2
    inv = ROPE_THETA ** (-jnp.arange(half, dtype=F32) / half)
    ang = pos.astype(F32)[:, None] * inv[None, :]
    cos, sin = jnp.cos(ang), jnp.sin(ang)
    pad = jnp.zeros((pos.shape[0], LANES - QK_ROPE), F32)
    return (jnp.concatenate([cos, cos, pad], axis=1), jnp.concatenate([-sin, sin, pad], axis=1))


def _rope_cols(w):
    half = QK_ROPE // 2
    z = jnp.zeros(w.shape[:-1] + (QK_ROPE,), w.dtype)
    return jnp.concatenate([w, z, w[..., half:], w[..., :half], z], axis=-1)


def _prep_layers(ln_in, w_in, ln_q, w_q_b, ln_kv, w_kv_b, conv_w, conv_b, ln_conv_g, ln_conv_b,
                 ln_out_a, ln_out_c, w_out, w_pe, ln_pg, w_pg):
    depth = w_in.shape[0]
    i1, i2, i3 = Q_LORA, Q_LORA + KV_LORA, Q_LORA + KV_LORA + QK_ROPE
    w_in_ext = jnp.concatenate([w_in[:, :, :i2], _rope_cols(w_in[:, :, i2:i3]), w_in[:, :, i3:]], axis=-1)

    wq4 = w_q_b.reshape(depth, Q_LORA, N_HEADS, QK_NOPE + QK_ROPE)
    wkv4 = w_kv_b.reshape(depth, KV_LORA, N_HEADS, QK_NOPE + V_HEAD)
    wq_nope = wq4[..., :QK_NOPE].transpose(0, 2, 1, 3)
    wuk_t = wkv4[..., :QK_NOPE].transpose(0, 2, 3, 1)
    w_fold = _fold_q_uk(wq_nope, wuk_t).transpose(0, 2, 1, 3).reshape(depth, Q_LORA, N_HEADS * KV_LORA)
    w_qr = _rope_cols(wq4[..., QK_NOPE:]).reshape(depth, Q_LORA, N_HEADS * LANES)
    wq = jnp.stack([w_fold.reshape(depth, Q_LORA, N_HEADS, LANES), w_qr.reshape(depth, Q_LORA, N_HEADS, LANES)],
                   axis=3).reshape(depth, Q_LORA, N_HEADS * HEAD_W)

    w_uv = wkv4[..., QK_NOPE:]
    z = jnp.zeros((depth, KV_LORA, N_HEADS // 2, V_HEAD), w_uv.dtype)
    top = jnp.concatenate([w_uv[:, :, 0::2], z], axis=-1)
    bot = jnp.concatenate([z, w_uv[:, :, 1::2]], axis=-1)
    wuv = jnp.concatenate([top, bot], axis=1).transpose(0, 2, 1, 3)

    row = lambda a: a[:, None, :]
    return [dict(ln_in=row(ln_in)[l], w_in=w_in_ext[l].astype(BF16), ln_q=row(ln_q)[l], wq=wq[l].astype(BF16),
                 ln_kv=row(ln_kv)[l], conv_w=conv_w[l], conv_b=row(conv_b)[l], ln_conv_g=row(ln_conv_g)[l],
                 ln_conv_b=row(ln_conv_b)[l], ln_out_c=row(ln_out_c)[l], ln_out_a=row(ln_out_a)[l],
                 wuv=wuv[l].astype(BF16), w_out=w_out[l].astype(BF16), ln_pg=row(ln_pg)[l],
                 w_pg=w_pg[l].astype(BF16), w_pe=w_pe[l].astype(BF16))
            for l in range(depth)]


def _pick(n, cands):
    for c in cands:
        if n % c == 0:
            return c
    return n


def kernel(x_prompt, x_sample, cache_ckv, cache_krope, state_conv, p_prompt, p_sample, ln_in, w_in, ln_q, w_q_b, ln_kv, w_kv_b, conv_w, conv_b, ln_conv_g, ln_conv_b, ln_out_a, ln_out_c, w_out, w_pe, ln_pg, w_pg, ln_f):
    depth = w_in.shape[0]
    bp, tp, _ = x_prompt.shape
    bs, tsmp, _ = x_sample.shape
    past = cache_ckv.shape[2]
    layers = _prep_layers(ln_in, w_in, ln_q, w_q_b, ln_kv, w_kv_b, conv_w, conv_b, ln_conv_g, ln_conv_b,
                          ln_out_a, ln_out_c, w_out, w_pe, ln_pg, w_pg)
    ln_f2 = ln_f[None, :]

    ts_p = _pick(tp, (1024, 512, 256, 128))
    tq_p = _pick(tp, (512, 256, 128))
    tk_p = tq_p
    tm_p = _pick(bp * tp, (1024, 512, 256, 128))
    cos_p, sin_p = _rope_tables(jnp.arange(tp))
    cos_s, sin_s = _rope_tables(past + jnp.arange(tsmp))
    cos_s, sin_s = jnp.tile(cos_s, (bs, 1)), jnp.tile(sin_s, (bs, 1))
    hist_p = jnp.zeros((bp, HIST, CONV_CH), F32)
    tk_s = _pick(past, (512, 256, 128))
    tl_s = -(-tsmp // LANES) * LANES
    ones_col = (jnp.arange(HEAD_W - KV_LORA - QK_ROPE) == 0).astype(F32)

    h_p = x_prompt
    h_s = x_sample
    outs = [[] for _ in range(6)]
    for l in range(depth):
        lw = layers[l]
        final = l == depth - 1
        q, kv, ckv, kr, ga, yc, tail = _proj_in(h_p, hist_p, cos_p, sin_p, lw, ts=ts_p)
        ya = _attend(q, kv, ga, lw["wuv"], lw["ln_out_a"], tq=tq_p, tk=tk_p, tl=tk_p, causal=True, kv_len=tp)
        h_p = _proj_out(ya.reshape(bp * tp, -1), yc.reshape(bp * tp, -1), h_p.reshape(bp * tp, -1),
                        p_prompt.reshape(depth, bp * tp, -1), l, lw, ln_f2, final=final, tm=tm_p).reshape(bp, tp, -1)
        outs[0].append(ckv)
        outs[1].append(kr)
        outs[2].append(tail[:, HIST - CONV_STATE:, :])
        hist_s = jnp.pad(state_conv[l], ((0, 0), (HIST - CONV_STATE, 0), (0, 0)))
        q, kv, ckv, kr, ga, yc, tail = _proj_in(h_s, hist_s, cos_s, sin_s, lw, ts=tsmp)
        kv_past = jnp.concatenate(
            [cache_ckv[l], cache_krope[l], jnp.broadcast_to(ones_col, (bs, past, ones_col.shape[0]))], axis=-1)
        kv_all = jnp.concatenate(
            [kv_past.astype(BF16), kv, jnp.zeros((bs, tl_s - tsmp, HEAD_W), BF16)], axis=1)
        ya = _attend(q, kv_all, ga, lw["wuv"], lw["ln_out_a"], tq=tsmp, tk=tk_s, tl=tl_s, causal=False,
                     kv_len=past + tsmp)
        h_s = _proj_out(ya.reshape(bs * tsmp, -1), yc.reshape(bs * tsmp, -1), h_s.reshape(bs * tsmp, -1),
                        p_sample.reshape(depth, bs * tsmp, -1), l, lw, ln_f2, final=final,
                        tm=bs * tsmp).reshape(bs, tsmp, -1)
        outs[3].append(ckv)
        outs[4].append(kr)
        outs[5].append(tail[:, HIST - CONV_STATE:, :])
    return (h_p, h_s) + tuple(jnp.stack(o) for o in outs)
```

```python
import functools
import math

import jax
import jax.numpy as jnp
from jax import lax
from jax.experimental import pallas as pl
from jax.experimental.pallas import tpu as pltpu

F32 = jnp.float32
BF16 = jnp.bfloat16

N_HEADS = 8
QK_NOPE = 64
QK_ROPE = 32
V_HEAD = 64
KV_LORA = 128
Q_LORA = 256
D_MODEL = 1024
CONV_CH = 512
WIDTH_A = N_HEADS * V_HEAD
CONV_W = 31
CONV_STATE = CONV_W - 1
CHUNK = 64
ROPE_THETA = 10000.0
EPS = 1e-6
NEG_INF = -1e30

LANES = 128
SUBLANES = 8
HIST = 32
HEAD_W = 2 * LANES
ONES_COL = KV_LORA + QK_ROPE
Q_SCALE = (QK_NOPE + QK_ROPE) ** -0.5 * math.log2(math.e)

C_Q, C_KV, C_KR, C_GA, C_UV, C_UG, C_GC, C_END = 0, 256, 384, 512, 1024, 1536, 2048, 2560

VMEM_LIMIT = 60 * 1024 * 1024
ROW_BLOCK = 128


def _rms(x, g):
    return x * lax.rsqrt(jnp.mean(x * x, axis=-1, keepdims=True) + EPS) * g


def _silu(x):
    return x * jax.nn.sigmoid(x)


def _rope(x, cos, sin):
    return x * cos + pltpu.roll(x, 64, 1) * sin


def _fold_kernel(wq_ref, wuk_ref, o_ref):
    o_ref[0, 0] = jnp.dot(wq_ref[0, 0], wuk_ref[0, 0], precision=lax.Precision.HIGHEST,
                          preferred_element_type=F32)


def _fold_q_uk(wq_nope, wuk_t):
    depth, heads = wq_nope.shape[:2]
    return pl.pallas_call(
        _fold_kernel,
        grid=(depth, heads),
        in_specs=[pl.BlockSpec((1, 1, Q_LORA, QK_NOPE), lambda l, h: (l, h, 0, 0)),
                  pl.BlockSpec((1, 1, QK_NOPE, KV_LORA), lambda l, h: (l, h, 0, 0))],
        out_specs=pl.BlockSpec((1, 1, Q_LORA, KV_LORA), lambda l, h: (l, h, 0, 0)),
        out_shape=jax.ShapeDtypeStruct((depth, heads, Q_LORA, KV_LORA), F32),
        name="fold_q_uk",
    )(wq_nope, wuk_t)


def _proj_in_kernel(h_ref, hist_ref, cos_ref, sin_ref, ln_in_ref, w_in_ref, ln_q_ref, wq_ref, ln_kv_ref,
                    cw_ref, cb_ref, lcg_ref, lcb_ref, loc_ref,
                    q_ref, kv_ref, ckv_ref, kr_ref, ga_ref, yc_ref, tail_ref,
                    ubuf_ref, sh_ref, gcs_ref, v_ref, wb_ref, *, nb, ts, rc):
    n = nb * ts

    @pl.when(pl.program_id(1) == 0)
    def _():
        ubuf_ref[:, 0:HIST, :] = hist_ref[...]

    x = h_ref[...].reshape(n, D_MODEL)
    xn = _rms(x, ln_in_ref[...]).astype(BF16)

    def seg(a, b):
        return jnp.dot(xn, w_in_ref[:, a:b], preferred_element_type=F32)

    u = seg(C_UV, C_UG) * jax.nn.sigmoid(seg(C_UG, C_GC))
    ubuf_ref[:, HIST:HIST + ts, :] = u.reshape(nb, ts, CONV_CH)

    cqn_box = []

    def cq_finish(z):
        cqn_box.append(_rms(z, ln_q_ref[...]).astype(BF16))

    def q_start(hh):
        return jnp.dot(cqn_box[0], wq_ref[:, hh * HEAD_W:(hh + 1) * HEAD_W], preferred_element_type=F32)

    def q_finish(hh, qh):
        q_ref[:, hh, :, 0:LANES] = (qh[:, 0:LANES] * Q_SCALE).astype(BF16).reshape(nb, ts, LANES)
        ro = _rope(qh[:, LANES:HEAD_W], cos_ref[...], sin_ref[...]) * Q_SCALE
        q_ref[:, hh, :, LANES:HEAD_W] = ro.astype(BF16).reshape(nb, ts, LANES)

    def kv_finish(z):
        ckvn = _rms(z[:, 0:KV_LORA], ln_kv_ref[...])
        ckv_ref[...] = ckvn.reshape(nb, ts, KV_LORA)
        kro = _rope(z[:, KV_LORA:], cos_ref[...], sin_ref[...])
        kr_ref[...] = kro[:, :QK_ROPE].reshape(nb, ts, QK_ROPE)
        lane = lax.broadcasted_iota(jnp.int32, (n, LANES), 1)
        kv_ref[:, :, 0:LANES] = ckvn.astype(BF16).reshape(nb, ts, LANES)
        kv_ref[:, :, LANES:HEAD_W] = jnp.where(lane == ONES_COL - LANES, 1.0, kro).astype(BF16).reshape(nb, ts, LANES)

    def ga_finish(j, z):
        ga_ref[:, :, j * HEAD_W:(j + 1) * HEAD_W] = _silu(z).astype(BF16).reshape(nb, ts, HEAD_W)

    def gc_finish(j, z):
        gcs_ref[:, j * HEAD_W:(j + 1) * HEAD_W] = _silu(z)

    part = functools.partial
    pieces = ([(part(seg, C_Q, C_KV), cq_finish), (part(seg, C_KV, C_GA), kv_finish)]
              + [(part(seg, C_GA + j * HEAD_W, C_GA + (j + 1) * HEAD_W), part(ga_finish, j))
                 for j in range(WIDTH_A // HEAD_W)]
              + [(part(seg, C_GC + j * HEAD_W, C_GC + (j + 1) * HEAD_W), part(gc_finish, j))
                 for j in range(CONV_CH // HEAD_W)]
              + [(part(q_start, hh), part(q_finish, hh)) for hh in range(N_HEADS)])

    for k in range(CONV_W):
        wb_ref[SUBLANES * k:SUBLANES * (k + 1), :] = jnp.broadcast_to(cw_ref[k:k + 1, :], (SUBLANES, CONV_CH))
    cb = jnp.broadcast_to(cb_ref[...], (SUBLANES, CONV_CH))
    ng = rc // SUBLANES
    for s in range(nb):
        for b in range(1, 8):
            sh_ref[b - 1] = ubuf_ref[s, pl.ds(b, ts + 24), :]

        def chunk(r, carry, s=s):
            r0 = r * rc if isinstance(r, int) else pl.multiple_of(r * rc, rc)
            accs = [cb] * ng
            for o in range(HIST - CONV_STATE, HIST + 1):
                a, b = divmod(o, 8)
                k = o - (HIST - CONV_STATE)
                w8 = wb_ref[SUBLANES * k:SUBLANES * (k + 1), :]
                for g in range(ng):
                    row = pl.ds(r0 + SUBLANES * (a + g), SUBLANES)
                    src = ubuf_ref[s, row, :] if b == 0 else sh_ref[b - 1, row, :]
                    accs[g] = accs[g] + w8 * src
            for g in range(ng):
                v_ref[pl.ds(s * ts + r0 + SUBLANES * g, SUBLANES), :] = accs[g]
            return carry

        if nb == 1:
            for r in range(ts // rc):
                z = pieces[r][0]() if r < len(pieces) else None
                chunk(r, 0)
                if r < len(pieces):
                    pieces[r][1](z)
            del pieces[:ts // rc]
        else:
            lax.fori_loop(0, ts // rc, chunk, 0)

    for start, finish in pieces:
        finish(start())

    v = v_ref[...]
    xc = v - jnp.mean(v, axis=-1, keepdims=True)
    var = jnp.mean(xc * xc, axis=-1, keepdims=True)
    v = _silu(xc * lax.rsqrt(var + EPS) * lcg_ref[...] + lcb_ref[...])
    yc_ref[...] = _rms(v * gcs_ref[...], loc_ref[...]).astype(BF16).reshape(nb, ts, CONV_CH)

    tail = ubuf_ref[:, ts:ts + HIST, :]
    tail_ref[...] = tail
    ubuf_ref[:, 0:HIST, :] = tail


def _proj_in(h, hist, cos, sin, lw, *, ts):
    bsz, t_len, _ = h.shape
    nb = bsz if t_len == ts and bsz * ts <= 512 else 1
    nt = t_len // ts
    rc = min(32, ts)
    const2 = lambda b, t: (0, 0)
    row3 = lambda b, t: (b, t, 0)
    seq3 = lambda b, t: (b, 0, 0)
    in_specs = [
        pl.BlockSpec((nb, ts, D_MODEL), row3),
        pl.BlockSpec((nb, HIST, CONV_CH), seq3),
        pl.BlockSpec((nb * ts, LANES), lambda b, t: (t, 0)),
        pl.BlockSpec((nb * ts, LANES), lambda b, t: (t, 0)),
        pl.BlockSpec((1, D_MODEL), const2),
        pl.BlockSpec((D_MODEL, C_END), const2),
        pl.BlockSpec((1, Q_LORA), const2),
        pl.BlockSpec((Q_LORA, 2 * N_HEADS * LANES), const2),
        pl.BlockSpec((1, KV_LORA), const2),
        pl.BlockSpec((CONV_W, CONV_CH), const2),
        pl.BlockSpec((1, CONV_CH), const2),
        pl.BlockSpec((1, CONV_CH), const2),
        pl.BlockSpec((1, CONV_CH), const2),
        pl.BlockSpec((1, CONV_CH), const2),
    ]
    out_shape = (
        jax.ShapeDtypeStruct((bsz, N_HEADS, t_len, HEAD_W), BF16),
        jax.ShapeDtypeStruct((bsz, t_len, HEAD_W), BF16),
        jax.ShapeDtypeStruct((bsz, t_len, KV_LORA), F32),
        jax.ShapeDtypeStruct((bsz, t_len, QK_ROPE), F32),
        jax.ShapeDtypeStruct((bsz, t_len, WIDTH_A), BF16),
        jax.ShapeDtypeStruct((bsz, t_len, CONV_CH), BF16),
        jax.ShapeDtypeStruct((bsz, HIST, CONV_CH), F32),
    )
    out_specs = (
        pl.BlockSpec((nb, N_HEADS, ts, HEAD_W), lambda b, t: (b, 0, t, 0)),
        pl.BlockSpec((nb, ts, HEAD_W), row3),
        pl.BlockSpec((nb, ts, KV_LORA), row3),
        pl.BlockSpec((nb, ts, QK_ROPE), row3),
        pl.BlockSpec((nb, ts, WIDTH_A), row3),
        pl.BlockSpec((nb, ts, CONV_CH), row3),
        pl.BlockSpec((nb, HIST, CONV_CH), seq3),
    )
    return pl.pallas_call(
        functools.partial(_proj_in_kernel, nb=nb, ts=ts, rc=rc),
        grid=(bsz // nb, nt),
        in_specs=in_specs,
        out_specs=out_specs,
        out_shape=out_shape,
        scratch_shapes=[pltpu.VMEM((nb, ts + HIST, CONV_CH), F32),
                        pltpu.VMEM((7, ts + 24, CONV_CH), F32),
                        pltpu.VMEM((nb * ts, CONV_CH), F32),
                        pltpu.VMEM((nb * ts, CONV_CH), F32),
                        pltpu.VMEM((CONV_W * SUBLANES, CONV_CH), F32)],
        compiler_params=pltpu.CompilerParams(dimension_semantics=("parallel", "arbitrary"),
                                             vmem_limit_bytes=VMEM_LIMIT),
        name="proj_in",
    )(h, hist, cos, sin, lw["ln_in"], lw["w_in"], lw["ln_q"], lw["wq"], lw["ln_kv"],
      lw["conv_w"], lw["conv_b"], lw["ln_conv_g"], lw["ln_conv_b"], lw["ln_out_c"])


def _attend_kernel(q_ref, qn_ref, kv_ref, ga_ref, wuv_ref, lna_ref, ya_ref, acc_ref, m_ref, sa_ref, sb_ref,
                   *, tq, tk, tl, causal, kv_len):
    rows = N_HEADS * tq
    rb = min(ROW_BLOCK, rows)
    q = q_ref[0].reshape(rows, HEAD_W)
    acc_ref[...] = jnp.zeros_like(acc_ref)
    m_ref[...] = jnp.full_like(m_ref, NEG_INF)

    def key_tile(j, width):
        k0 = j * tk
        if not isinstance(k0, int):
            k0 = pl.multiple_of(k0, tk)
        return kv_ref[0, pl.ds(k0, width), :]

    def qk(qv, j, s_ref, width):
        s_ref[:, 0:width] = lax.dot_general(qv, key_tile(j, width), (((1,), (1,)), ((), ())),
                                            preferred_element_type=F32)

    if causal:
        assert tq == tk == tl and rb == LANES and LANES == 2 * CHUNK and tq % rb == 0
        drow = lax.broadcasted_iota(jnp.int32, (rb, LANES), 0)
        dlane = lax.broadcasted_iota(jnp.int32, (rb, LANES), 1)
        diag_mask = dlane < (drow | (CHUNK - 1)) + 1
    else:
        tail_mask = lax.broadcasted_iota(jnp.int32, (rb, LANES), 1) < kv_len - (kv_len // tk) * tk

    def softmax_pv(s_ref, j, width, masked):
        nc = width // LANES
        ps, alphas = [], []
        for r in range(rows // rb):
            sl = slice(r * rb, (r + 1) * rb)
            nvis = ((r * rb) % tq) // LANES + 1 if (masked and causal) else nc
            sc = [s_ref[sl, c * LANES:(c + 1) * LANES] for c in range(nvis)]
            if masked:
                sc[-1] = jnp.where(diag_mask if causal else tail_mask, sc[-1], NEG_INF)
            mx = functools.reduce(jnp.maximum, sc)
            m_prev = m_ref[sl, :]
            m_new = jnp.maximum(m_prev, jnp.max(mx, axis=1, keepdims=True))
            m_ref[sl, :] = m_new
            alphas.append(jnp.exp2(m_prev - m_new))
            p = [jnp.exp2(x - m_new).astype(BF16) for x in sc] + [jnp.zeros((rb, LANES), BF16)] * (nc - nvis)
            ps.append(jnp.concatenate(p, axis=1))
        pv = jnp.dot(jnp.concatenate(ps, axis=0), key_tile(j, width), preferred_element_type=F32)
        for r in range(rows // rb):
            sl = slice(r * rb, (r + 1) * rb)
            for c in range(HEAD_W // LANES):
                cl = slice(c * LANES, (c + 1) * LANES)
                acc_ref[sl, cl] = alphas[r] * acc_ref[sl, cl] + pv[sl, cl]

    if causal:
        i = pl.program_id(1)
        n = i
        qn = qn_ref[0].reshape(rows, HEAD_W)

        @pl.when(i == 0)
        def _():
            qk(q, 0, sa_ref, tk)

        def run(b0, b1):
            def pair(jj, carry):
                j = 2 * jj
                qk(q, j + 1, b1, tk)
                softmax_pv(b0, j, tk, False)
                qk(q, j + 2, b0, tk)
                softmax_pv(b1, j + 1, tk, False)
                return carry

            lax.fori_loop(0, n // 2, pair, 0)
            odd = (n & 1) == 1

            @pl.when(odd)
            def _():
                qk(q, n, b1, tk)
                softmax_pv(b0, n - 1, tk, False)
                qk(qn, 0, b0, tk)
                softmax_pv(b1, n, tk, True)

            @pl.when(jnp.logical_not(odd))
            def _():
                qk(qn, 0, b1, tk)
                softmax_pv(b0, n, tk, True)

        swapped = (((i + 1) >> 1) & 1) == 1

        @pl.when(jnp.logical_not(swapped))
        def _():
            run(sa_ref, sb_ref)

        @pl.when(swapped)
        def _():
            run(sb_ref, sa_ref)
    else:
        n_full = kv_len // tk
        bufs = (sa_ref, sb_ref)
        qk(q, 0, sa_ref, tk if n_full else tl)
        for j in range(n_full):
            qk(q, j + 1, bufs[(j + 1) % 2], tk if j + 1 < n_full else tl)
            softmax_pv(bufs[j % 2], j, tk, False)
        softmax_pv(bufs[n_full % 2], n_full, tl, True)

    acc = acc_ref[...]
    lane = lax.broadcasted_iota(jnp.int32, (rows, LANES), 1)
    l = jnp.sum(jnp.where(lane == ONES_COL - LANES, acc[:, LANES:HEAD_W], 0.0), axis=1, keepdims=True)
    o = (acc[:, 0:LANES] / l).astype(BF16)
    cols = []
    for j in range(N_HEADS // 2):
        pair_lhs = jnp.concatenate([o[(2 * j) * tq:(2 * j + 1) * tq], o[(2 * j + 1) * tq:(2 * j + 2) * tq]], axis=1)
        cols.append(jnp.dot(pair_lhs, wuv_ref[j], preferred_element_type=F32))
    o_a = jnp.concatenate(cols, axis=1)
    ya_ref[0] = _rms(o_a * ga_ref[0].astype(F32), lna_ref[...]).astype(BF16)


def _attend(q, kv, ga, wuv, ln_out_a, *, tq, tk, tl, causal, kv_len):
    bsz, _, t_len, _ = q.shape
    s_len = kv.shape[1]
    rows = N_HEADS * tq
    nq = t_len // tq
    return pl.pallas_call(
        functools.partial(_attend_kernel, tq=tq, tk=tk, tl=tl, causal=causal, kv_len=kv_len),
        grid=(bsz, nq),
        in_specs=[pl.BlockSpec((1, N_HEADS, tq, HEAD_W), lambda b, i: (b, 0, i, 0)),
                  pl.BlockSpec((1, N_HEADS, tq, HEAD_W), lambda b, i: (b, 0, jnp.minimum(i + 1, nq - 1), 0)),
                  pl.BlockSpec((1, s_len, HEAD_W), lambda b, i: (b, 0, 0)),
                  pl.BlockSpec((1, tq, WIDTH_A), lambda b, i: (b, i, 0)),
                  pl.BlockSpec((N_HEADS // 2, HEAD_W, LANES), lambda b, i: (0, 0, 0)),
                  pl.BlockSpec((1, WIDTH_A), lambda b, i: (0, 0))],
        out_specs=pl.BlockSpec((1, tq, WIDTH_A), lambda b, i: (b, i, 0)),
        out_shape=jax.ShapeDtypeStruct((bsz, t_len, WIDTH_A), BF16),
        scratch_shapes=[pltpu.VMEM((rows, HEAD_W), F32), pltpu.VMEM((rows, LANES), F32),
                        pltpu.VMEM((rows, tk), F32), pltpu.VMEM((rows, tk), F32)],
        compiler_params=pltpu.CompilerParams(dimension_semantics=("arbitrary", "arbitrary"),
                                             vmem_limit_bytes=VMEM_LIMIT),
        name="attend",
    )(q, q, kv, ga, wuv, ln_out_a)


def _proj_out_kernel(ya_ref, yc_ref, h_ref, p_ref, wo_ref, lnpg_ref, wpg_ref, wpe_ref, lnf_ref, out_ref, *, final):
    h1 = (h_ref[...]
          + jnp.dot(ya_ref[...], wo_ref[0:WIDTH_A, :], preferred_element_type=F32)
          + jnp.dot(yc_ref[...], wo_ref[WIDTH_A:, :], preferred_element_type=F32))
    gate = jax.nn.sigmoid(jnp.dot(_rms(h1, lnpg_ref[...]).astype(BF16), wpg_ref[...], preferred_element_type=F32))
    pe = jnp.dot(p_ref[0].astype(BF16), wpe_ref[...], preferred_element_type=F32)
    h2 = h1 + gate * pe
    out_ref[...] = _rms(h2, lnf_ref[...]) if final else h2


def _proj_out(ya, yc, h, p, layer, lw, ln_f, *, final, tm):
    n = h.shape[0]
    d_ple = p.shape[2]
    row = lambda i: (i, 0)
    const = lambda i: (0, 0)
    return pl.pallas_call(
        functools.partial(_proj_out_kernel, final=final),
        grid=(n // tm,),
        in_specs=[pl.BlockSpec((tm, WIDTH_A), row),
                  pl.BlockSpec((tm, CONV_CH), row),
                  pl.BlockSpec((tm, D_MODEL), row),
                  pl.BlockSpec((1, tm, d_ple), lambda i: (layer, i, 0)),
                  pl.BlockSpec((WIDTH_A + CONV_CH, D_MODEL), const),
                  pl.BlockSpec((1, D_MODEL), const),
                  pl.BlockSpec((D_MODEL, D_MODEL), const),
                  pl.BlockSpec((d_ple, D_MODEL), const),
                  pl.BlockSpec((1, D_MODEL), const)],
        out_specs=pl.BlockSpec((tm, D_MODEL), row),
        out_shape=jax.ShapeDtypeStruct((n, D_MODEL), F32),
        compiler_params=pltpu.CompilerParams(dimension_semantics=("parallel",),
                                             vmem_limit_bytes=VMEM_LIMIT),
        name="proj_out",
    )(ya, yc, h, p, lw["w_out"], lw["ln_pg"], lw["w_pg"], lw["w_pe"], ln_f)


def _rope_tables(pos):
    half = QK_ROPE // 2
    inv = ROPE_THETA ** (-jnp.arange(half, dtype=F32) / half)
    ang = pos.astype(F32)[:, None] * inv[None, :]
    cos, sin = jnp.cos(ang), jnp.sin(ang)
    pad = jnp.zeros((pos.shape[0], LANES - QK_ROPE), F32)
    return (jnp.concatenate([cos, cos, pad], axis=1), jnp.concatenate([-sin, sin, pad], axis=1))


def _rope_cols(w):
    half = QK_ROPE // 2
    z = jnp.zeros(w.shape[:-1] + (QK_ROPE,), w.dtype)
    return jnp.concatenate([w, z, w[..., half:], w[..., :half], z], axis=-1)


def _prep_layers(ln_in, w_in, ln_q, w_q_b, ln_kv, w_kv_b, conv_w, conv_b, ln_conv_g, ln_conv_b,
                 ln_out_a, ln_out_c, w_out, w_pe, ln_pg, w_pg):
    depth = w_in.shape[0]
    i1, i2, i3 = Q_LORA, Q_LORA + KV_LORA, Q_LORA + KV_LORA + QK_ROPE
    w_in_ext = jnp.concatenate([w_in[:, :, :i2], _rope_cols(w_in[:, :, i2:i3]), w_in[:, :, i3:]], axis=-1)

    wq4 = w_q_b.reshape(depth, Q_LORA, N_HEADS, QK_NOPE + QK_ROPE)
    wkv4 = w_kv_b.reshape(depth, KV_LORA, N_HEADS, QK_NOPE + V_HEAD)
    wq_nope = wq4[..., :QK_NOPE].transpose(0, 2, 1, 3)
    wuk_t = wkv4[..., :QK_NOPE].transpose(0, 2, 3, 1)
    w_fold = _fold_q_uk(wq_nope, wuk_t).transpose(0, 2, 1, 3).reshape(depth, Q_LORA, N_HEADS * KV_LORA)
    w_qr = _rope_cols(wq4[..., QK_NOPE:]).reshape(depth, Q_LORA, N_HEADS * LANES)
    wq = jnp.stack([w_fold.reshape(depth, Q_LORA, N_HEADS, LANES), w_qr.reshape(depth, Q_LORA, N_HEADS, LANES)],
                   axis=3).reshape(depth, Q_LORA, N_HEADS * HEAD_W)

    w_uv = wkv4[..., QK_NOPE:]
    z = jnp.zeros((depth, KV_LORA, N_HEADS // 2, V_HEAD), w_uv.dtype)
    top = jnp.concatenate([w_uv[:, :, 0::2], z], axis=-1)
    bot = jnp.concatenate([z, w_uv[:, :, 1::2]], axis=-1)
    wuv = jnp.concatenate([top, bot], axis=1).transpose(0, 2, 1, 3)

    row = lambda a: a[:, None, :]
    return [dict(ln_in=row(ln_in)[l], w_in=w_in_ext[l].astype(BF16), ln_q=row(ln_q)[l], wq=wq[l].astype(BF16),
                 ln_kv=row(ln_kv)[l], conv_w=conv_w[l], conv_b=row(conv_b)[l], ln_conv_g=row(ln_conv_g)[l],
                 ln_conv_b=row(ln_conv_b)[l], ln_out_c=row(ln_out_c)[l], ln_out_a=row(ln_out_a)[l],
                 wuv=wuv[l].astype(BF16), w_out=w_out[l].astype(BF16), ln_pg=row(ln_pg)[l],
                 w_pg=w_pg[l].astype(BF16), w_pe=w_pe[l].astype(BF16))
            for l in range(depth)]


def _pick(n, cands):
    for c in cands:
        if n % c == 0:
            return c
    return n


def kernel(x_prompt, x_sample, cache_ckv, cache_krope, state_conv, p_prompt, p_sample, ln_in, w_in, ln_q, w_q_b, ln_kv, w_kv_b, conv_w, conv_b, ln_conv_g, ln_conv_b, ln_out_a, ln_out_c, w_out, w_pe, ln_pg, w_pg, ln_f):
    depth = w_in.shape[0]
    bp, tp, _ = x_prompt.shape
    bs, tsmp, _ = x_sample.shape
    past = cache_ckv.shape[2]
    layers = _prep_layers(ln_in, w_in, ln_q, w_q_b, ln_kv, w_kv_b, conv_w, conv_b, ln_conv_g, ln_conv_b,
                          ln_out_a, ln_out_c, w_out, w_pe, ln_pg, w_pg)
    ln_f2 = ln_f[None, :]

    ts_p = _pick(tp, (1024, 512, 256, 128))
    tq_p = _pick(tp, (512, 256, 128))
    tk_p = tq_p
    tm_p = _pick(bp * tp, (1024, 512, 256, 128))
    cos_p, sin_p = _rope_tables(jnp.arange(tp))
    cos_s, sin_s = _rope_tables(past + jnp.arange(tsmp))
    cos_s, sin_s = jnp.tile(cos_s, (bs, 1)), jnp.tile(sin_s, (bs, 1))
    hist_p = jnp.zeros((bp, HIST, CONV_CH), F32)
    tk_s = _pick(past, (512, 256, 128))
    tl_s = -(-tsmp // LANES) * LANES
    ones_col = (jnp.arange(HEAD_W - KV_LORA - QK_ROPE) == 0).astype(F32)

    h_p = x_prompt
    h_s = x_sample
    outs = [[] for _ in range(6)]
    for l in range(depth):
        lw = layers[l]
        final = l == depth - 1
        q, kv, ckv, kr, ga, yc, tail = _proj_in(h_p, hist_p, cos_p, sin_p, lw, ts=ts_p)
        ya = _attend(q, kv, ga, lw["wuv"], lw["ln_out_a"], tq=tq_p, tk=tk_p, tl=tk_p, causal=True, kv_len=tp)
        h_p = _proj_out(ya.reshape(bp * tp, -1), yc.reshape(bp * tp, -1), h_p.reshape(bp * tp, -1),
                        p_prompt.reshape(depth, bp * tp, -1), l, lw, ln_f2, final=final, tm=tm_p).reshape(bp, tp, -1)
        outs[0].append(ckv)
        outs[1].append(kr)
        outs[2].append(tail[:, HIST - CONV_STATE:, :])
        hist_s = jnp.pad(state_conv[l], ((0, 0), (HIST - CONV_STATE, 0), (0, 0)))
        q, kv, ckv, kr, ga, yc, tail = _proj_in(h_s, hist_s, cos_s, sin_s, lw, ts=tsmp)
        kv_past = jnp.concatenate(
            [cache_ckv[l], cache_krope[l], jnp.broadcast_to(ones_col, (bs, past, ones_col.shape[0]))], axis=-1)
        kv_all = jnp.concatenate(
            [kv_past.astype(BF16), kv, jnp.zeros((bs, tl_s - tsmp, HEAD_W), BF16)], axis=1)
        ya = _attend(q, kv_all, ga, lw["wuv"], lw["ln_out_a"], tq=tsmp, tk=tk_s, tl=tl_s, causal=False,
                     kv_len=past + tsmp)
        h_s = _proj_out(ya.reshape(bs * tsmp, -1), yc.reshape(bs * tsmp, -1), h_s.reshape(bs * tsmp, -1),
                        p_sample.reshape(depth, bs * tsmp, -1), l, lw, ln_f2, final=final,
                        tm=bs * tsmp).reshape(bs, tsmp, -1)
        outs[3].append(ckv)
        outs[4].append(kr)
        outs[5].append(tail[:, HIST - CONV_STATE:, :])
    return (h_p, h_s) + tuple(jnp.stack(o) for o in outs)
```
